```python
import math
import jax, jax.numpy as jnp
from jax import lax
import numpy as np

D_MODEL = 2048
BATCH = 1
SEQ = 8192
DEPTH = 4

HEAD_DIM = 128
N_SELF_HEADS = 12
N_MEM_HEADS = 4
SELF_WIDTH = N_SELF_HEADS * HEAD_DIM
MEM_WIDTH = N_MEM_HEADS * HEAD_DIM
IN_WIDTH = 3 * SELF_WIDTH + MEM_WIDTH
N_MEM = 256
D_FF = 5632
MOBA_BLOCK = 256
MOBA_TOP_K = 3
MOBA_Q_CHUNK = 64
DILATED_CONFIGS = ((128, 1), (512, 4), (2048, 16))
HEADS_PER_DIL_GROUP = N_SELF_HEADS // len(DILATED_CONFIGS)
N_MIXERS = 2
RMS_EPS = 1e-6
NEG_INF = -1e30

kernel_name = "hybrid_moba_dilated_macaron_trunk"


def rmsnorm(x, g):
    xf = x.astype(jnp.float32)
    y = xf * lax.rsqrt(jnp.mean(xf * xf, axis=-1, keepdims=True) + RMS_EPS)
    return (y * g.astype(jnp.float32)).astype(x.dtype)


def swiglu(x, w_in, w_out):
    gate, up = jnp.split(x @ w_in, 2, axis=-1)
    return (jax.nn.silu(gate) * up) @ w_out


def alibi_slopes(n):
    return jnp.exp2(-8.0 * jnp.arange(1, n + 1, dtype=jnp.float32) / n)


def moba_attention(q, k, v, slopes):
    B, S, H, dh = q.shape
    nb = -(-S // MOBA_BLOCK)
    Sp = nb * MOBA_BLOCK
    pad = ((0, 0), (0, Sp - S), (0, 0), (0, 0))
    q, k, v = [jnp.pad(a, pad).transpose(0, 2, 1, 3) for a in (q, k, v)]
    scale = HEAD_DIM ** -0.5
    kb = k.reshape(B, H, nb, MOBA_BLOCK, dh)
    vb = v.reshape(B, H, nb, MOBA_BLOCK, dh)
    k_mean = jnp.mean(kb.astype(jnp.float32), axis=3)
    gate = jnp.einsum('bhsd,bhnd->bhsn', q.astype(jnp.float32), k_mean)
    q_block = jnp.arange(Sp) // MOBA_BLOCK
    past = jnp.arange(nb)[None, :] < q_block[:, None]
    gate = jnp.where(past, gate, NEG_INF)
    top_k = min(MOBA_TOP_K, nb)
    _, sel = lax.top_k(gate, top_k)
    sel_valid = sel < q_block[:, None]

    n_chunks = Sp // MOBA_Q_CHUNK
    QC = MOBA_Q_CHUNK
    q_c = q.reshape(B, H, n_chunks, QC, dh).transpose(2, 0, 1, 3, 4)
    sel_c = sel.reshape(B, H, n_chunks, QC, top_k).transpose(2, 0, 1, 3, 4)
    val_c = sel_valid.reshape(B, H, n_chunks, QC, top_k).transpose(2, 0, 1, 3, 4)
    starts = jnp.arange(n_chunks, dtype=jnp.int32) * QC
    bi = jnp.arange(B)[:, None, None, None]
    hi = jnp.arange(H)[None, :, None, None]
    offs = jnp.arange(MOBA_BLOCK, dtype=jnp.int32)
    n_sel = top_k * MOBA_BLOCK

    def chunk(args):
        qc, sc, vc, t0 = args
        t = t0 + jnp.arange(QC, dtype=jnp.int32)
        k_sel = kb[bi, hi, sc]
        v_sel = vb[bi, hi, sc]
        s_sel = jnp.einsum('bhqd,bhqkjd->bhqkj', qc, k_sel).astype(jnp.float32) * scale
        key_pos = sc[..., None] * MOBA_BLOCK + offs
        dist = (t[:, None, None] - key_pos).astype(jnp.float32)
        s_sel = s_sel - slopes[:, None, None, None] * dist
        s_sel = jnp.where(vc[..., None], s_sel, NEG_INF)
        own = t0 // MOBA_BLOCK
        k_own = lax.dynamic_index_in_dim(kb, own, axis=2, keepdims=False)
        v_own = lax.dynamic_index_in_dim(vb, own, axis=2, keepdims=False)
        d_own = t[:, None] - (own * MOBA_BLOCK + offs)[None, :]
        s_own = jnp.einsum('bhqd,bhjd->bhqj', qc, k_own).astype(jnp.float32) * scale
        s_own = s_own - slopes[:, None, None] * d_own.astype(jnp.float32)
        s_own = jnp.where(d_own >= 0, s_own, NEG_INF)
        scores = jnp.concatenate([s_sel.reshape(B, H, QC, n_sel), s_own], axis=-1)
        p = jax.nn.softmax(scores, axis=-1).astype(v.dtype)
        p_sel = p[..., :n_sel].reshape(B, H, QC, top_k, MOBA_BLOCK)
        p_own = p[..., n_sel:]
        return (jnp.einsum('bhqkj,bhqkjd->bhqd', p_sel, v_sel)
                + jnp.einsum('bhqj,bhjd->bhqd', p_own, v_own))

    out = lax.map(chunk, (q_c, sel_c, val_c, starts))
    out = out.transpose(1, 0, 3, 2, 4).reshape(B, Sp, H, dh)
    return out[:, :S]


def dilated_group(q, k, v, slopes, window, dilation):
    B, S, Hg, dh = q.shape
    L = window // dilation
    span = L * dilation
    Sp = -(-S // span) * span
    n = Sp // dilation
    nblk = n // L
    scale = HEAD_DIM ** -0.5

    def to_blocks(a):
        a = jnp.pad(a, ((0, 0), (0, Sp - S), (0, 0), (0, 0)))
        a = a.reshape(B, n, dilation, Hg, dh).transpose(0, 2, 3, 1, 4)
        return a.reshape(B, dilation, Hg, nblk, L, dh)

    def with_prev(a):
        prev = jnp.pad(a[:, :, :, :-1], ((0, 0), (0, 0), (0, 0), (1, 0), (0, 0), (0, 0)))
        return jnp.concatenate([prev, a], axis=4)

    qb = to_blocks(q)
    k2 = with_prev(to_blocks(k))
    v2 = with_prev(to_blocks(v))
    scores = jnp.einsum('brhcqd,brhckd->brhcqk', qb, k2).astype(jnp.float32) * scale
    delta = L + jnp.arange(L)[:, None] - jnp.arange(2 * L)[None, :]
    key_idx = jnp.arange(nblk)[:, None] * L - L + jnp.arange(2 * L)[None, :]
    mask = ((delta >= 0) & (delta <= L))[None, :, :] & (key_idx >= 0)[:, None, :]
    scores = scores - slopes[:, None, None, None] * (delta * dilation).astype(jnp.float32)
    scores = jnp.where(mask, scores, NEG_INF)
    lse = jax.nn.logsumexp(scores, axis=-1)
    p = jnp.exp(scores - lse[..., None]).astype(v.dtype)
    out = jnp.einsum('brhcqk,brhckd->brhcqd', p, v2)
    out = out.reshape(B, dilation, Hg, n, dh).transpose(0, 3, 1, 2, 4).reshape(B, Sp, Hg, dh)
    lse = lse.reshape(B, dilation, Hg, n).transpose(0, 3, 1, 2).reshape(B, Sp, Hg)
    return out[:, :S], lse[:, :S]


def dilated_attention(q, k, v, slopes):
    outs, lses = [], []
    for g, (window, dilation) in enumerate(DILATED_CONFIGS):
        hs = slice(g * HEADS_PER_DIL_GROUP, (g + 1) * HEADS_PER_DIL_GROUP)
        o, l = dilated_group(q[:, :, hs], k[:, :, hs], v[:, :, hs], slopes[hs], window, dilation)
        outs.append(o)
        lses.append(l)
    alpha = jax.nn.softmax(jnp.stack(lses, axis=0), axis=0)
    return jnp.concatenate(
        [alpha[g][..., None].astype(outs[g].dtype) * outs[g] for g in range(len(outs))], axis=2)


def token_mixing(h, mem_n, w_in, w_mem_kv, w_out, kind, slopes):
    B, S, _ = h.shape
    proj = h @ w_in
    q = proj[..., :SELF_WIDTH].reshape(B, S, N_SELF_HEADS, HEAD_DIM)
    k = proj[..., SELF_WIDTH:2 * SELF_WIDTH].reshape(B, S, N_SELF_HEADS, HEAD_DIM)
    v = proj[..., 2 * SELF_WIDTH:3 * SELF_WIDTH].reshape(B, S, N_SELF_HEADS, HEAD_DIM)
    qm = proj[..., 3 * SELF_WIDTH:].reshape(B, S, N_MEM_HEADS, HEAD_DIM)
    if kind == 0:
        self_out = moba_attention(q, k, v, slopes)
    else:
        self_out = dilated_attention(q, k, v, slopes)
    km, vm = jnp.split(mem_n @ w_mem_kv, 2, axis=-1)
    km = km.reshape(B, N_MEM, N_MEM_HEADS, HEAD_DIM)
    vm = vm.reshape(B, N_MEM, N_MEM_HEADS, HEAD_DIM)
    s = jnp.einsum('bshd,bmhd->bhsm', qm, km).astype(jnp.float32) * (HEAD_DIM ** -0.5)
    p = jax.nn.softmax(s, axis=-1).astype(vm.dtype)
    mem_out = jnp.einsum('bhsm,bmhd->bshd', p, vm)
    cat = jnp.concatenate([self_out.reshape(B, S, SELF_WIDTH), mem_out.reshape(B, S, MEM_WIDTH)], axis=-1)
    return cat @ w_out


def setup_inputs(seed: int = 0) -> dict:
    key = jax.random.key(seed)
    ks = jax.random.split(key, 10)
    f32 = jnp.float32
    x = jax.random.normal(ks[0], (BATCH, SEQ, D_MODEL), f32)
    mem = jax.random.normal(ks[1], (BATCH, N_MEM, D_MODEL), f32)
    g_pre = 1.0 + 0.05 * jax.random.normal(ks[2], (DEPTH, 3, D_MODEL), f32)
    g_post = 1.0 + 0.05 * jax.random.normal(ks[3], (DEPTH, 3, D_MODEL), f32)
    g_mem = 1.0 + 0.05 * jax.random.normal(ks[4], (DEPTH, D_MODEL), f32)
    w_ffn_in = jax.random.normal(ks[5], (DEPTH, 2, D_MODEL, 2 * D_FF), f32) * D_MODEL ** -0.5
    w_ffn_out = jax.random.normal(ks[6], (DEPTH, 2, D_FF, D_MODEL), f32) * D_FF ** -0.5
    w_in = jax.random.normal(ks[7], (DEPTH, D_MODEL, IN_WIDTH), f32) * D_MODEL ** -0.5
    w_mem_kv = jax.random.normal(ks[8], (DEPTH, D_MODEL, 2 * MEM_WIDTH), f32) * D_MODEL ** -0.5
    w_out = jax.random.normal(ks[9], (DEPTH, SELF_WIDTH + MEM_WIDTH, D_MODEL), f32) * (SELF_WIDTH + MEM_WIDTH) ** -0.5
    return {"x": x, "mem": mem, "g_pre": g_pre, "g_post": g_post, "g_mem": g_mem,
            "w_ffn_in": w_ffn_in, "w_ffn_out": w_ffn_out, "w_in": w_in,
            "w_mem_kv": w_mem_kv, "w_out": w_out}


def reference(x, mem, g_pre, g_post, g_mem, w_ffn_in, w_ffn_out, w_in, w_mem_kv, w_out):
    slopes = alibi_slopes(N_SELF_HEADS)
    for layer in range(DEPTH):
        kind = layer % N_MIXERS
        f = swiglu(rmsnorm(x, g_pre[layer, 0]), w_ffn_in[layer, 0], w_ffn_out[layer, 0])
        x = x + 0.5 * rmsnorm(f, g_post[layer, 0])
        mem_n = rmsnorm(mem, g_mem[layer])
        m = token_mixing(rmsnorm(x, g_pre[layer, 1]), mem_n, w_in[layer], w_mem_kv[layer],
                         w_out[layer], kind, slopes)
        x = x + rmsnorm(m, g_post[layer, 1])
        f = swiglu(rmsnorm(x, g_pre[layer, 2]), w_ffn_in[layer, 1], w_ffn_out[layer, 1])
        x = x + 0.5 * rmsnorm(f, g_post[layer, 2])
    return x
```

```python
import functools

import jax
import jax.numpy as jnp
from jax import lax
from jax.experimental import pallas as pl
from jax.experimental.pallas import tpu as pltpu

D_MODEL = 2048
SEQ = 8192
DEPTH = 4
HEAD_DIM = 128
N_SELF_HEADS = 12
N_MEM_HEADS = 4
SELF_WIDTH = N_SELF_HEADS * HEAD_DIM
MEM_WIDTH = N_MEM_HEADS * HEAD_DIM
IN_WIDTH = 3 * SELF_WIDTH + MEM_WIDTH
N_MEM = 256
D_FF = 5632
MOBA_BLOCK = 256
MOBA_TOP_K = 3
DILATED_CONFIGS = ((128, 1), (512, 4), (2048, 16))
N_GROUPS = len(DILATED_CONFIGS)
HEADS_PER_GROUP = N_SELF_HEADS // N_GROUPS
GROUP_WIDTH = HEADS_PER_GROUP * HEAD_DIM
RMS_EPS = 1e-6
NEG_INF = -1e30
SCALE = HEAD_DIM ** -0.5

Q_BLOCK = 256
N_QBLOCKS = SEQ // Q_BLOCK
N_KBLOCKS = SEQ // MOBA_BLOCK
V7X_VMEM_LIMIT = 56 * 1024 * 1024

F32 = jnp.float32
BF16 = jnp.bfloat16


def _params(semantics):
    return pltpu.CompilerParams(dimension_semantics=semantics, vmem_limit_bytes=V7X_VMEM_LIMIT)


def _rms(x, g):
    return x * lax.rsqrt(jnp.mean(x * x, axis=-1, keepdims=True) + RMS_EPS) * g


def _rmsnorm_kernel(x_ref, g_ref, o_ref):
    o_ref[...] = _rms(x_ref[...], g_ref[...]).astype(BF16)


def rmsnorm_bf16(x, g, tm):
    m, d = x.shape
    return pl.pallas_call(
        _rmsnorm_kernel,
        grid=(m // tm,),
        in_specs=[pl.BlockSpec((tm, d), lambda i: (i, 0)),
                  pl.BlockSpec((1, d), lambda i: (0, 0))],
        out_specs=pl.BlockSpec((tm, d), lambda i: (i, 0)),
        out_shape=jax.ShapeDtypeStruct((m, d), BF16),
        compiler_params=_params(("arbitrary",)),
        name="rmsnorm",
    )(x, g.reshape(1, d))


def _ffn_in_kernel(x_ref, wg_ref, wu_ref, o_ref):
    x = x_ref[...]
    gate = jnp.dot(x, wg_ref[...], preferred_element_type=F32)
    up = jnp.dot(x, wu_ref[...], preferred_element_type=F32)
    o_ref[...] = (gate * jax.nn.sigmoid(gate) * up).astype(BF16)


def ffn_in(xn, w, tm=2048, tn=512):
    m, d = xn.shape
    nj = D_FF // tn
    return pl.pallas_call(
        _ffn_in_kernel,
        grid=(m // tm, nj),
        in_specs=[pl.BlockSpec((tm, d), lambda i, j: (i, 0)),
                  pl.BlockSpec((d, tn), lambda i, j: (0, j)),
                  pl.BlockSpec((d, tn), lambda i, j: (0, j + nj))],
        out_specs=pl.BlockSpec((tm, tn), lambda i, j: (i, j)),
        out_shape=jax.ShapeDtypeStruct((m, D_FF), BF16),
        compiler_params=_params(("arbitrary", "arbitrary")),
        name="ffn_in",
    )(xn, w, w)


def _residual_epilogue(f, x_ref, gpost_ref, gnext_ref, xo_ref, xn_ref, coef):
    xnew = x_ref[...] + coef * _rms(f, gpost_ref[...])
    xo_ref[...] = xnew
    xn_ref[...] = _rms(xnew, gnext_ref[...]).astype(BF16)


def _ffn_out_kernel(a_ref, w_ref, x_ref, gpost_ref, gnext_ref, xo_ref, xn_ref, *, coef):
    f = jnp.dot(a_ref[...], w_ref[...], preferred_element_type=F32)
    _residual_epilogue(f, x_ref, gpost_ref, gnext_ref, xo_ref, xn_ref, coef)


def ffn_out(a, w, x, g_post, g_next, coef, tm=256):
    m, k = a.shape
    d = w.shape[1]
    row = lambda i: (i, 0)
    fixed = lambda i: (0, 0)
    return pl.pallas_call(
        functools.partial(_ffn_out_kernel, coef=coef),
        grid=(m // tm,),
        in_specs=[pl.BlockSpec((tm, k), row),
                  pl.BlockSpec((k, d), fixed, pipeline_mode=pl.Buffered(1)),
                  pl.BlockSpec((tm, d), row),
                  pl.BlockSpec((1, d), fixed),
                  pl.BlockSpec((1, d), fixed)],
        out_specs=[pl.BlockSpec((tm, d), row), pl.BlockSpec((tm, d), row)],
        out_shape=[jax.ShapeDtypeStruct((m, d), F32), jax.ShapeDtypeStruct((m, d), BF16)],
        compiler_params=_params(("arbitrary",)),
        name="ffn_out",
    )(a, w, x, g_post.reshape(1, d), g_next.reshape(1, d))


def _out_proj_kernel(s_ref, m_ref, w_ref, x_ref, gpost_ref, gnext_ref, xo_ref, xn_ref):
    cat = jnp.concatenate([s_ref[g] for g in range(N_GROUPS)] + [m_ref[...]], axis=-1)
    f = jnp.dot(cat, w_ref[...], preferred_element_type=F32)
    _residual_epilogue(f, x_ref, gpost_ref, gnext_ref, xo_ref, xn_ref, 1.0)


def out_proj(self_out, mem_out, w, x, g_post, g_next, tm=256):
    m = x.shape[0]
    d = w.shape[1]
    row = lambda i: (i, 0)
    fixed = lambda i: (0, 0)
    return pl.pallas_call(
        _out_proj_kernel,
        grid=(m // tm,),
        in_specs=[pl.BlockSpec((N_GROUPS, tm, GROUP_WIDTH), lambda i: (0, i, 0)),
                  pl.BlockSpec((tm, MEM_WIDTH), row),
                  pl.BlockSpec((SELF_WIDTH + MEM_WIDTH, d), fixed, pipeline_mode=pl.Buffered(1)),
                  pl.BlockSpec((tm, d), row),
                  pl.BlockSpec((1, d), fixed),
                  pl.BlockSpec((1, d), fixed)],
        out_specs=[pl.BlockSpec((tm, d), row), pl.BlockSpec((tm, d), row)],
        out_shape=[jax.ShapeDtypeStruct((m, d), F32), jax.ShapeDtypeStruct((m, d), BF16)],
        compiler_params=_params(("arbitrary",)),
        name="out_proj",
    )(self_out, mem_out, w, x, g_post.reshape(1, d), g_next.reshape(1, d))


def _head_proj_kernel(x_ref, w_ref, o_ref):
    r = jnp.dot(x_ref[...], w_ref[...], preferred_element_type=F32).astype(BF16)
    for c in range(o_ref.shape[0]):
        o_ref[c] = r[:, c * HEAD_DIM:(c + 1) * HEAD_DIM]


def head_proj(xn, w, tm, tn=512):
    m, d = xn.shape
    n = w.shape[1]
    hp = tn // HEAD_DIM
    return pl.pallas_call(
        _head_proj_kernel,
        grid=(m // tm, n // tn),
        in_specs=[pl.BlockSpec((tm, d), lambda i, j: (i, 0)),
                  pl.BlockSpec((d, tn), lambda i, j: (0, j))],
        out_specs=pl.BlockSpec((hp, tm, HEAD_DIM), lambda i, j: (j, i, 0)),
        out_shape=jax.ShapeDtypeStruct((n // HEAD_DIM, m, HEAD_DIM), BF16),
        compiler_params=_params(("arbitrary", "arbitrary")),
        name="head_proj",
    )(xn, w)


def _qk(q, k):
    return lax.dot_general(q, k, (((1,), (1,)), ((), ())), preferred_element_type=F32)


def _online_update(s, v, m, l, acc):
    m_new = jnp.maximum(m, jnp.max(s, axis=-1, keepdims=True))
    alpha = jnp.exp(m - m_new)
    p = jnp.exp(s - m_new)
    l = alpha * l + jnp.sum(p, axis=-1, keepdims=True)
    acc = alpha * acc + jnp.dot(p.astype(BF16), v, preferred_element_type=F32)
    return m_new, l, acc


def _row_minus_col():
    r = lax.broadcasted_iota(jnp.int32, (Q_BLOCK, MOBA_BLOCK), 0)
    c = lax.broadcasted_iota(jnp.int32, (Q_BLOCK, MOBA_BLOCK), 1)
    return r - c


def _moba_kernel(slopes_ref, q_ref, k_ref, v_ref, o_ref, kmean_ref):
    h = pl.program_id(0)
    qb = pl.program_id(1)
    slope = slopes_ref[h]

    @pl.when(qb == 0)
    def _():
        def mean_block(j, carry):
            rows = pl.ds(pl.multiple_of(j * MOBA_BLOCK, MOBA_BLOCK), MOBA_BLOCK)
            kmean_ref[pl.ds(j, 1), :] = jnp.mean(k_ref[rows, :].astype(F32), axis=0, keepdims=True)
            return carry
        lax.fori_loop(0, N_KBLOCKS, mean_block, 0)

    q = q_ref[...]
    gate = _qk(q, kmean_ref[...].astype(BF16))
    col = lax.broadcasted_iota(jnp.int32, (Q_BLOCK, N_KBLOCKS), 1)
    past = col < qb
    gate = jnp.where(past, gate, NEG_INF)
    rank = jnp.zeros((Q_BLOCK, N_KBLOCKS), jnp.int32)
    for n in range(N_KBLOCKS):
        gn = gate[:, n:n + 1]
        beats = (gn > gate) | ((gn == gate) & (n < col))
        rank = rank + beats.astype(jnp.int32)
    sel_bias = jnp.where((rank < MOBA_TOP_K) & past, 0.0, NEG_INF)

    dist0 = _row_minus_col()
    alibi0 = slope * dist0.astype(F32)

    row0 = pl.multiple_of(qb * MOBA_BLOCK, MOBA_BLOCK)
    s = _qk(q, k_ref[pl.ds(row0, MOBA_BLOCK), :]) * SCALE - alibi0
    s = jnp.where(dist0 >= 0, s, NEG_INF)
    m = jnp.max(s, axis=-1, keepdims=True)
    p = jnp.exp(s - m)
    l = jnp.sum(p, axis=-1, keepdims=True)
    acc = jnp.dot(p.astype(BF16), v_ref[pl.ds(row0, MOBA_BLOCK), :], preferred_element_type=F32)

    def body(j, carry):
        m, l, acc = carry
        rows = pl.ds(pl.multiple_of(j * MOBA_BLOCK, MOBA_BLOCK), MOBA_BLOCK)
        block_bias = jnp.sum(jnp.where(col == j, sel_bias, 0.0), axis=-1, keepdims=True)
        block_bias = block_bias - slope * ((qb - j) * MOBA_BLOCK).astype(F32)
        s = _qk(q, k_ref[rows, :]) * SCALE - alibi0 + block_bias
        return _online_update(s, v_ref[rows, :], m, l, acc)

    m, l, acc = lax.fori_loop(0, qb, body, (m, l, acc))
    o_ref[...] = (acc / l).astype(BF16)


def moba_attention(proj, slopes):
    h_ = N_SELF_HEADS
    return pl.pallas_call(
        _moba_kernel,
        grid=(h_, N_QBLOCKS),
        in_specs=[pl.BlockSpec(memory_space=pltpu.SMEM),
                  pl.BlockSpec((None, Q_BLOCK, HEAD_DIM), lambda h, i: (h, i, 0)),
                  pl.BlockSpec((None, SEQ, HEAD_DIM), lambda h, i: (h_ + h, 0, 0)),
                  pl.BlockSpec((None, SEQ, HEAD_DIM), lambda h, i: (2 * h_ + h, 0, 0))],
        out_specs=pl.BlockSpec((None, Q_BLOCK, HEAD_DIM),
                               lambda h, i: (h // HEADS_PER_GROUP, i, h % HEADS_PER_GROUP)),
        out_shape=jax.ShapeDtypeStruct((N_GROUPS, SEQ, GROUP_WIDTH), BF16),
        scratch_shapes=[pltpu.VMEM((N_KBLOCKS, HEAD_DIM), F32)],
        compiler_params=_params(("arbitrary", "arbitrary")),
        name="moba_attn",
    )(slopes, proj, proj, proj)


def _dilated_group(q, k_ref, v_ref, slope, qb, window, dilation, dist0):
    n_back = max(window // MOBA_BLOCK, 1)

    def scores(j, valid):
        rows = pl.ds(pl.multiple_of(j * MOBA_BLOCK, MOBA_BLOCK), MOBA_BLOCK)
        dist = dist0 + (qb - j) * MOBA_BLOCK
        ok = (dist >= 0) & (dist <= window) & valid
        if dilation > 1:
            ok = ok & ((dist & (dilation - 1)) == 0)
        s = _qk(q, k_ref[rows, :]) * SCALE - slope * dist.astype(F32)
        return jnp.where(ok, s, NEG_INF), v_ref[rows, :]

    s, v = scores(qb, True)
    m = jnp.max(s, axis=-1, keepdims=True)
    p = jnp.exp(s - m)
    l = jnp.sum(p, axis=-1, keepdims=True)
    acc = jnp.dot(p.astype(BF16), v, preferred_element_type=F32)

    def body(t, carry):
        j = qb - 1 - t
        s, v = scores(jnp.maximum(j, 0), j >= 0)
        return _online_update(s, v, *carry)

    m, l, acc = lax.fori_loop(0, n_back, body, (m, l, acc))
    return acc / l, m + jnp.log(l)


def _dilated_kernel(slopes_ref, *refs):
    q_refs, k_refs, v_refs, o_ref = refs[0:3], refs[3:6], refs[6:9], refs[9]
    hg = pl.program_id(0)
    qb = pl.program_id(1)
    dist0 = _row_minus_col()
    outs, lses = [], []
    for g, (window, dilation) in enumerate(DILATED_CONFIGS):
        slope = slopes_ref[g * HEADS_PER_GROUP + hg]
        o, lse = _dilated_group(q_refs[g][...], k_refs[g], v_refs[g], slope, qb, window, dilation, dist0)
        outs.append(o)
        lses.append(lse)
    mx = functools.reduce(jnp.maximum, lses)
    es = [jnp.exp(lse - mx) for lse in lses]
    tot = functools.reduce(lambda a, b: a + b, es)
    for g in range(N_GROUPS):
        o_ref[g] = ((es[g] / tot) * outs[g]).astype(BF16)


def dilated_attention(proj, slopes):
    h_ = N_SELF_HEADS
    hpg = HEADS_PER_GROUP

    def q_spec(g):
        return pl.BlockSpec((None, Q_BLOCK, HEAD_DIM), lambda hg, i: (g * hpg + hg, i, 0))

    def kv_spec(g, base):
        return pl.BlockSpec((None, SEQ, HEAD_DIM), lambda hg, i: (base + g * hpg + hg, 0, 0))

    in_specs = ([pl.BlockSpec(memory_space=pltpu.SMEM)]
                + [q_spec(g) for g in range(N_GROUPS)]
                + [kv_spec(g, h_) for g in range(N_GROUPS)]
                + [kv_spec(g, 2 * h_) for g in range(N_GROUPS)])
    return pl.pallas_call(
        _dilated_kernel,
        grid=(hpg, N_QBLOCKS),
        in_specs=in_specs,
        out_specs=pl.BlockSpec((N_GROUPS, Q_BLOCK, HEAD_DIM), lambda hg, i: (0, i, hg)),
        out_shape=jax.ShapeDtypeStruct((N_GROUPS, SEQ, GROUP_WIDTH), BF16),
        compiler_params=_params(("arbitrary", "arbitrary")),
        name="dilated_attn",
    )(slopes, *([proj] * 9))


def _mem_attn_kernel(q_ref, k_ref, v_ref, o_ref):
    s = _qk(q_ref[...], k_ref[...]) * SCALE
    m = jnp.max(s, axis=-1, keepdims=True)
    p = jnp.exp(s - m)
    l = jnp.sum(p, axis=-1, keepdims=True)
    acc = jnp.dot(p.astype(BF16), v_ref[...], preferred_element_type=F32)
    o_ref[...] = (acc / l).astype(BF16)


def mem_attention(proj, mem_kv, tq=1024):
    base = 3 * N_SELF_HEADS
    return pl.pallas_call(
        _mem_attn_kernel,
        grid=(N_MEM_HEADS, SEQ // tq),
        in_specs=[pl.BlockSpec((None, tq, HEAD_DIM), lambda h, i: (base + h, i, 0)),
                  pl.BlockSpec((None, N_MEM, HEAD_DIM), lambda h, i: (h, 0, 0)),
                  pl.BlockSpec((None, N_MEM, HEAD_DIM), lambda h, i: (N_MEM_HEADS + h, 0, 0))],
        out_specs=pl.BlockSpec((tq, HEAD_DIM), lambda h, i: (i, h)),
        out_shape=jax.ShapeDtypeStruct((SEQ, MEM_WIDTH), BF16),
        compiler_params=_params(("arbitrary", "arbitrary")),
        name="mem_attn",
    )(proj, mem_kv, mem_kv)


def kernel(x, mem, g_pre, g_post, g_mem, w_ffn_in, w_ffn_out, w_in, w_mem_kv, w_out):
    slopes = jnp.exp2(-8.0 * jnp.arange(1, N_SELF_HEADS + 1, dtype=F32) / N_SELF_HEADS)
    xs = x.reshape(SEQ, D_MODEL)
    mem2 = mem.reshape(N_MEM, D_MODEL)
    w_ffn_in = w_ffn_in.astype(BF16)
    w_ffn_out = w_ffn_out.astype(BF16)
    w_in = w_in.astype(BF16)
    w_mem_kv = w_mem_kv.astype(BF16)
    w_out = w_out.astype(BF16)

    xn = rmsnorm_bf16(xs, g_pre[0, 0], tm=512)
    for layer in range(DEPTH):
        h = ffn_in(xn, w_ffn_in[layer, 0])
        xs, xn = ffn_out(h, w_ffn_out[layer, 0], xs, g_post[layer, 0], g_pre[layer, 1], 0.5)

        proj = head_proj(xn, w_in[layer], tm=2048)
        mem_n = rmsnorm_bf16(mem2, g_mem[layer], tm=N_MEM)
        mem_kv = head_proj(mem_n, w_mem_kv[layer], tm=N_MEM)
        if layer % 2 == 0:
            self_out = moba_attention(proj, slopes)
        else:
            self_out = dilated_attention(proj, slopes)
        mem_out = mem_attention(proj, mem_kv)
        xs, xn = out_proj(self_out, mem_out, w_out[layer], xs, g_post[layer, 1], g_pre[layer, 2])

        h = ffn_in(xn, w_ffn_in[layer, 1])
        g_next = g_pre[layer + 1, 0] if layer + 1 < DEPTH else g_pre[layer, 0]
        xs, xn = ffn_out(h, w_ffn_out[layer, 1], xs, g_post[layer, 2], g_next, 0.5)
    return xs.reshape(x.shape)
```

```python
import functools

import jax
import jax.numpy as jnp
from jax import lax
from jax.experimental import pallas as pl
from jax.experimental.pallas import tpu as pltpu

D_MODEL = 2048
SEQ = 8192
DEPTH = 4
HEAD_DIM = 128
N_SELF_HEADS = 12
N_MEM_HEADS = 4
SELF_WIDTH = N_SELF_HEADS * HEAD_DIM
MEM_WIDTH = N_MEM_HEADS * HEAD_DIM
IN_WIDTH = 3 * SELF_WIDTH + MEM_WIDTH
N_MEM = 256
D_FF = 5632
MOBA_BLOCK = 256
MOBA_TOP_K = 3
DILATED_CONFIGS = ((128, 1), (512, 4), (2048, 16))
N_GROUPS = len(DILATED_CONFIGS)
HEADS_PER_GROUP = N_SELF_HEADS // N_GROUPS
GROUP_WIDTH = HEADS_PER_GROUP * HEAD_DIM
RMS_EPS = 1e-6
NEG_INF = -1e30
SCALE = HEAD_DIM ** -0.5

BLK = MOBA_BLOCK
N_BLOCKS = SEQ // BLK
MOBA_HEADS_PER_STEP = 2
VT_ROWS = HEAD_DIM + 16
LOG2E = 1.4426950408889634
LN2 = 0.6931471805599453
KAUG_WIDTH = 2 * HEAD_DIM
ONEHOT_COL = HEAD_DIM
OFFSET_COL = HEAD_DIM + N_BLOCKS
PROJ_TN = 512
HEADS_PER_TILE = PROJ_TN // HEAD_DIM
V7X_VMEM_LIMIT = 56 * 1024 * 1024

F32 = jnp.float32
BF16 = jnp.bfloat16
NT_DIMS = (((1,), (1,)), ((), ()))


def _params(semantics):
    return pltpu.CompilerParams(dimension_semantics=semantics, vmem_limit_bytes=V7X_VMEM_LIMIT)


def _rms(x, g):
    return x * lax.rsqrt(jnp.mean(x * x, axis=-1, keepdims=True) + RMS_EPS) * g


def _nt(a, b):
    return lax.dot_general(a, b, NT_DIMS, preferred_element_type=F32)


def _rmsnorm_kernel(x_ref, g_ref, o_ref):
    o_ref[...] = _rms(x_ref[...], g_ref[...]).astype(BF16)


def rmsnorm_bf16(x, g, tm):
    m, d = x.shape
    return pl.pallas_call(
        _rmsnorm_kernel,
        grid=(m // tm,),
        in_specs=[pl.BlockSpec((tm, d), lambda i: (i, 0)),
                  pl.BlockSpec((1, d), lambda i: (0, 0))],
        out_specs=pl.BlockSpec((tm, d), lambda i: (i, 0)),
        out_shape=jax.ShapeDtypeStruct((m, d), BF16),
        compiler_params=_params(("arbitrary",)),
        name="rmsnorm",
    )(x, g.reshape(1, d))


def _ffn_in_kernel(x_ref, wg_ref, wu_ref, o_ref):
    x = x_ref[...]
    gate = jnp.dot(x, wg_ref[...].astype(BF16), preferred_element_type=F32)
    up = jnp.dot(x, wu_ref[...].astype(BF16), preferred_element_type=F32)
    o_ref[...] = (gate * jax.nn.sigmoid(gate) * up).astype(BF16)


def ffn_in(xn, w_all, layer, which, tm=2048, tn=512):
    m, d = xn.shape
    nj = D_FF // tn
    return pl.pallas_call(
        _ffn_in_kernel,
        grid=(m // tm, nj),
        in_specs=[pl.BlockSpec((tm, d), lambda i, j: (i, 0)),
                  pl.BlockSpec((None, None, d, tn), lambda i, j: (layer, which, 0, j)),
                  pl.BlockSpec((None, None, d, tn), lambda i, j: (layer, which, 0, j + nj))],
        out_specs=pl.BlockSpec((tm, tn), lambda i, j: (i, j)),
        out_shape=jax.ShapeDtypeStruct((m, D_FF), BF16),
        compiler_params=_params(("arbitrary", "arbitrary")),
        name="ffn_in",
    )(xn, w_all, w_all)


def _residual_epilogue(f, x_ref, gpost_ref, gnext_ref, xo_ref, xn_ref, coef):
    xnew = x_ref[...] + coef * _rms(f, gpost_ref[...])
    xo_ref[...] = xnew
    xn_ref[...] = _rms(xnew, gnext_ref[...]).astype(BF16)


def _ffn_out_kernel(a_ref, w_ref, x_ref, gpost_ref, gnext_ref, xo_ref, xn_ref, *, coef):
    f = jnp.dot(a_ref[...], w_ref[...], preferred_element_type=F32)
    _residual_epilogue(f, x_ref, gpost_ref, gnext_ref, xo_ref, xn_ref, coef)


def ffn_out(a, w_all, layer, which, x, g_post, g_next, coef, tm=256):
    m, k = a.shape
    d = w_all.shape[-1]
    row = lambda i: (i, 0)
    fixed = lambda i: (0, 0)
    return pl.pallas_call(
        functools.partial(_ffn_out_kernel, coef=coef),
        grid=(m // tm,),
        in_specs=[pl.BlockSpec((tm, k), row),
                  pl.BlockSpec((None, None, k, d), lambda i: (layer, which, 0, 0),
                               pipeline_mode=pl.Buffered(1)),
                  pl.BlockSpec((tm, d), row),
                  pl.BlockSpec((1, d), fixed),
                  pl.BlockSpec((1, d), fixed)],
        out_specs=[pl.BlockSpec((tm, d), row), pl.BlockSpec((tm, d), row)],
        out_shape=[jax.ShapeDtypeStruct((m, d), F32), jax.ShapeDtypeStruct((m, d), BF16)],
        compiler_params=_params(("arbitrary",)),
        name="ffn_out",
    )(a, w_all, x, g_post.reshape(1, d), g_next.reshape(1, d))


def _out_proj_kernel(s_ref, m_ref, w_ref, x_ref, gpost_ref, gnext_ref, xo_ref, xn_ref):
    cat = jnp.concatenate([s_ref[g] for g in range(N_GROUPS)] + [m_ref[...]], axis=-1)
    f = jnp.dot(cat, w_ref[...], preferred_element_type=F32)
    _residual_epilogue(f, x_ref, gpost_ref, gnext_ref, xo_ref, xn_ref, 1.0)


def out_proj(self_out, mem_out, w_all, layer, x, g_post, g_next, tm=256):
    m = x.shape[0]
    d = w_all.shape[-1]
    row = lambda i: (i, 0)
    fixed = lambda i: (0, 0)
    return pl.pallas_call(
        _out_proj_kernel,
        grid=(m // tm,),
        in_specs=[pl.BlockSpec((N_GROUPS, tm, GROUP_WIDTH), lambda i: (0, i, 0)),
                  pl.BlockSpec((tm, MEM_WIDTH), row),
                  pl.BlockSpec((None, SELF_WIDTH + MEM_WIDTH, d), lambda i: (layer, 0, 0),
                               pipeline_mode=pl.Buffered(1)),
                  pl.BlockSpec((tm, d), row),
                  pl.BlockSpec((1, d), fixed),
                  pl.BlockSpec((1, d), fixed)],
        out_specs=[pl.BlockSpec((tm, d), row), pl.BlockSpec((tm, d), row)],
        out_shape=[jax.ShapeDtypeStruct((m, d), F32), jax.ShapeDtypeStruct((m, d), BF16)],
        compiler_params=_params(("arbitrary",)),
        name="out_proj",
    )(self_out, mem_out, w_all, x, g_post.reshape(1, d), g_next.reshape(1, d))


Q_TILES = SELF_WIDTH // PROJ_TN
K_TILE0 = Q_TILES
V_TILE0 = 2 * Q_TILES
MQ_TILE = 3 * Q_TILES
N_PROJ_TILES = IN_WIDTH // PROJ_TN


def _qkv_proj_kernel(x_ref, w_ref, q_ref, k_ref, vt_ref):
    i = pl.program_id(0)
    j = pl.program_id(1)
    tm = x_ref.shape[0]
    r = jnp.dot(x_ref[...], w_ref[...].astype(BF16), preferred_element_type=F32)

    @pl.when((j < K_TILE0) | (j == MQ_TILE))
    def _():
        for c in range(HEADS_PER_TILE):
            q_ref[c] = r[:, c * HEAD_DIM:(c + 1) * HEAD_DIM].astype(BF16)

    @pl.when((j >= K_TILE0) & (j < V_TILE0))
    def _():
        pos = i * tm + lax.broadcasted_iota(jnp.int32, (tm, HEAD_DIM), 0)
        lane = lax.broadcasted_iota(jnp.int32, (tm, HEAD_DIM), 1)
        is_offset = (lane >= OFFSET_COL - HEAD_DIM) & (lane < OFFSET_COL - HEAD_DIM + 3)
        blk_of_pos = lax.shift_right_logical(pos, BLK.bit_length() - 1)
        aug = jnp.where(lane == blk_of_pos, 1.0, jnp.where(is_offset, (pos & (BLK - 1)).astype(F32), 0.0))
        aug = aug.astype(BF16)
        for c in range(HEADS_PER_TILE):
            k_ref[c, :, 0:HEAD_DIM] = r[:, c * HEAD_DIM:(c + 1) * HEAD_DIM].astype(BF16)
            k_ref[c, :, HEAD_DIM:KAUG_WIDTH] = aug

    @pl.when((j >= V_TILE0) & (j < MQ_TILE))
    def _():
        for c in range(HEADS_PER_TILE):
            for b in range(tm // BLK):
                blk = r[b * BLK:(b + 1) * BLK, c * HEAD_DIM:(c + 1) * HEAD_DIM]
                vt_ref[c, b, 0:HEAD_DIM, :] = blk.T.astype(BF16)
                vt_ref[c, b, HEAD_DIM:VT_ROWS, :] = jnp.ones((VT_ROWS - HEAD_DIM, BLK), BF16)


def qkv_proj(xn, w_all, layer, tm=2048):
    m, d = xn.shape

    def q_map(i, j):
        return (jnp.where(j < K_TILE0, j, jnp.where(j < MQ_TILE, K_TILE0 - 1, K_TILE0)), i, 0)

    def k_map(i, j):
        return (jnp.clip(j - K_TILE0, 0, Q_TILES - 1), i, 0)

    def v_map(i, j):
        return (jnp.clip(j - V_TILE0, 0, Q_TILES - 1), i, 0, 0)

    hp = HEADS_PER_TILE
    return pl.pallas_call(
        _qkv_proj_kernel,
        grid=(m // tm, N_PROJ_TILES),
        in_specs=[pl.BlockSpec((tm, d), lambda i, j: (i, 0)),
                  pl.BlockSpec((None, d, PROJ_TN), lambda i, j: (layer, 0, j))],
        out_specs=[pl.BlockSpec((hp, tm, HEAD_DIM), q_map),
                   pl.BlockSpec((hp, tm, KAUG_WIDTH), k_map),
                   pl.BlockSpec((hp, tm // BLK, VT_ROWS, BLK), v_map)],
        out_shape=[jax.ShapeDtypeStruct((N_SELF_HEADS + N_MEM_HEADS, m, HEAD_DIM), BF16),
                   jax.ShapeDtypeStruct((N_SELF_HEADS, m, KAUG_WIDTH), BF16),
                   jax.ShapeDtypeStruct((N_SELF_HEADS, m // BLK, VT_ROWS, BLK), BF16)],
        compiler_params=_params(("arbitrary", "arbitrary")),
        name="qkv_proj",
    )(xn, w_all)


def _head_proj_kernel(x_ref, w_ref, o_ref):
    r = jnp.dot(x_ref[...], w_ref[...].astype(BF16), preferred_element_type=F32).astype(BF16)
    for c in range(o_ref.shape[0]):
        o_ref[c] = r[:, c * HEAD_DIM:(c + 1) * HEAD_DIM]


def mem_kv_proj(mem_n, w_all, layer):
    m, d = mem_n.shape
    n = w_all.shape[-1]
    return pl.pallas_call(
        _head_proj_kernel,
        grid=(n // PROJ_TN,),
        in_specs=[pl.BlockSpec((m, d), lambda j: (0, 0)),
                  pl.BlockSpec((None, d, PROJ_TN), lambda j: (layer, 0, j))],
        out_specs=pl.BlockSpec((HEADS_PER_TILE, m, HEAD_DIM), lambda j: (j, 0, 0)),
        out_shape=jax.ShapeDtypeStruct((n // HEAD_DIM, m, HEAD_DIM), BF16),
        compiler_params=_params(("arbitrary",)),
        name="mem_kv_proj",
    )(mem_n, w_all)


def _split3(x):
    hi = x.astype(BF16).astype(F32)
    mid = (x - hi).astype(BF16).astype(F32)
    lo = (x - hi - mid).astype(BF16).astype(F32)
    return hi, mid, lo


def _moba_prepare(q, kmean, slope, qb):
    gate_t = _nt(kmean.astype(BF16), q)
    blk_i = lax.broadcasted_iota(jnp.int32, (N_BLOCKS, BLK), 0)
    past = blk_i < qb
    gate_t = jnp.where(past, gate_t, NEG_INF)
    tiles = [gate_t[8 * g:8 * (g + 1), :] for g in range(N_BLOCKS // 8)]
    ranks = [jnp.zeros((8, BLK), jnp.int32) for _ in tiles]
    sub_i = lax.broadcasted_iota(jnp.int32, (8, BLK), 0)
    for n in range(N_BLOCKS):
        n_tile, n_sub = divmod(n, 8)
        gn = jnp.broadcast_to(tiles[n_tile][n_sub:n_sub + 1, :], (8, BLK))
        for g, tile in enumerate(tiles):
            if g > n_tile:
                beats = gn >= tile
            elif g < n_tile:
                beats = gn > tile
            else:
                beats = (gn > tile) | ((gn == tile) & (sub_i > n_sub))
            ranks[g] = ranks[g] + beats.astype(jnp.int32)
    sel = (jnp.concatenate(ranks, axis=0) < MOBA_TOP_K) & past
    bias_t = jnp.where(sel, 0.0, NEG_INF / SCALE)
    hi, mid, lo = _split3(jnp.full((8, BLK), slope / SCALE, F32))
    sub = lax.broadcasted_iota(jnp.int32, (8, BLK), 0)
    parts = jnp.where(sub == 0, hi, jnp.where(sub == 1, mid, jnp.where(sub == 2, lo, 0.0)))
    pad = jnp.zeros((HEAD_DIM - N_BLOCKS - 8, BLK), F32)
    aug = jnp.concatenate([bias_t, parts, pad], axis=0).T
    return jnp.concatenate([q, aug.astype(BF16)], axis=1)


def _moba_own_block(q, k_own, vt_own, slope):
    key_i = lax.broadcasted_iota(jnp.int32, (BLK, BLK), 0)
    qry_i = lax.broadcasted_iota(jnp.int32, (BLK, BLK), 1)
    s = SCALE * _nt(k_own, q) + slope * key_i.astype(F32)
    s = jnp.where(key_i <= qry_i, s, NEG_INF)
    m = jnp.max(s, axis=0, keepdims=True)
    p = jnp.exp(s - m)
    return m, jnp.dot(vt_own, p.astype(BF16), preferred_element_type=F32)


def _moba_qk(q_aug, k_ref, blk, raw_ref):
    rows = pl.ds(pl.multiple_of(blk * BLK, BLK), BLK)
    raw_ref[...] = _nt(k_ref[rows, :], q_aug)


def _moba_softmax(raw_ref, p_ref, blk, slope, qb, m):
    shift = slope * ((blk - qb) * BLK).astype(F32)
    m_new = jnp.maximum(m, SCALE * jnp.max(raw_ref[...], axis=0, keepdims=True) + shift)
    alpha = jnp.exp(m - m_new)
    p = jnp.exp2((SCALE * LOG2E) * raw_ref[...] - LOG2E * (m_new - shift))
    p_ref[...] = p.astype(BF16)
    return m_new, alpha


def _moba_pv(p_ref, vt_ref, blk, alpha, acc_ref):
    acc_ref[...] = alpha * acc_ref[...] + jnp.dot(vt_ref[blk], p_ref[...], preferred_element_type=F32)


def _moba_kernel(slopes_ref, q_ref, k_ref, vt_ref, o_ref, kmean_ref, raw_ref, p_ref, acc_ref):
    hp = pl.program_id(0)
    qb = pl.program_id(1)
    nh = MOBA_HEADS_PER_STEP
    last_blk = N_BLOCKS - 1

    @pl.when(qb == 0)
    def _():
        for hh in range(nh):
            def mean_block(j, carry):
                rows = pl.ds(pl.multiple_of(j * BLK, BLK), BLK)
                kmean_ref[hh, pl.ds(j, 1), :] = jnp.mean(
                    k_ref[hh, rows, 0:HEAD_DIM].astype(F32), axis=0, keepdims=True)
                return carry
            lax.fori_loop(0, N_BLOCKS, mean_block, 0)

    own_rows = pl.ds(pl.multiple_of(qb * BLK, BLK), BLK)
    slopes, q_augs, states = [], [], []
    for hh in range(nh):
        slope = slopes_ref[hp * nh + hh]
        q = q_ref[hh]
        q_aug = _moba_prepare(q, kmean_ref[hh], slope, qb)
        m, acc_ref[hh] = _moba_own_block(q, k_ref[hh, own_rows, 0:HEAD_DIM], vt_ref[hh, qb], slope)
        _moba_qk(q_aug, k_ref.at[hh], 0, raw_ref.at[0, hh])
        p_ref[1, hh] = jnp.zeros((BLK, BLK), BF16)
        states.extend([m, jnp.ones((1, BLK), F32)])
        slopes.append(slope)
        q_augs.append(q_aug)

    def body(c, carry):
        out = []
        for hh in range(nh):
            m, alpha_odd = carry[2 * hh:2 * hh + 2]
            k_h, vt_h, acc_h = k_ref.at[hh], vt_ref.at[hh], acc_ref.at[hh]
            even, odd = 2 * c, 2 * c + 1
            _moba_pv(p_ref.at[1, hh], vt_h, jnp.maximum(even - 1, 0), alpha_odd, acc_h)
            m, alpha_even = _moba_softmax(raw_ref.at[0, hh], p_ref.at[0, hh], even, slopes[hh], qb, m)
            _moba_qk(q_augs[hh], k_h, odd, raw_ref.at[1, hh])
            _moba_pv(p_ref.at[0, hh], vt_h, even, alpha_even, acc_h)
            m, alpha_odd = _moba_softmax(raw_ref.at[1, hh], p_ref.at[1, hh], odd, slopes[hh], qb, m)
            _moba_qk(q_augs[hh], k_h, jnp.minimum(even + 2, last_blk), raw_ref.at[0, hh])
            out.extend([m, alpha_odd])
        return tuple(out)

    n_pairs = (qb + 1) // 2
    states = lax.fori_loop(0, n_pairs, body, tuple(states))
    for hh in range(nh):
        alpha_odd = states[2 * hh + 1]
        _moba_pv(p_ref.at[1, hh], vt_ref.at[hh], jnp.maximum(2 * n_pairs - 1, 0), alpha_odd, acc_ref.at[hh])
        out_t = acc_ref[hh, 0:HEAD_DIM, :] / acc_ref[hh, HEAD_DIM:HEAD_DIM + 1, :]
        o_ref[:, hh * HEAD_DIM:(hh + 1) * HEAD_DIM] = out_t.T.astype(BF16)


def moba_attention(q_heads, k_aug, v_t, slopes):
    nh = MOBA_HEADS_PER_STEP
    steps_per_group = HEADS_PER_GROUP // nh
    return pl.pallas_call(
        _moba_kernel,
        grid=(N_SELF_HEADS // nh, N_BLOCKS),
        in_specs=[pl.BlockSpec(memory_space=pltpu.SMEM),
                  pl.BlockSpec((nh, BLK, HEAD_DIM), lambda h, i: (h, i, 0)),
                  pl.BlockSpec((nh, SEQ, KAUG_WIDTH), lambda h, i: (h, 0, 0)),
                  pl.BlockSpec((nh, N_BLOCKS, VT_ROWS, BLK), lambda h, i: (h, 0, 0, 0))],
        out_specs=pl.BlockSpec((None, BLK, nh * HEAD_DIM),
                               lambda h, i: (h // steps_per_group, i, h % steps_per_group)),
        out_shape=jax.ShapeDtypeStruct((N_GROUPS, SEQ, GROUP_WIDTH), BF16),
        scratch_shapes=[pltpu.VMEM((nh, N_BLOCKS, HEAD_DIM), F32),
                        pltpu.VMEM((2, nh, BLK, BLK), F32),
                        pltpu.VMEM((2, nh, BLK, BLK), BF16),
                        pltpu.VMEM((nh, VT_ROWS, BLK), F32)],
        compiler_params=_params(("arbitrary", "arbitrary")),
        name="moba_attn",
    )(slopes, q_heads, k_aug, v_t)


def _blocks_back(window):
    return max(window // BLK, 1)


def _dilated_group(q, k_ref, vt_ref, bias_ref, s_ref, qb, n_back):
    m2, v_blocks = None, []
    for t in range(n_back + 1):
        j = qb - n_back + t
        jc = jnp.maximum(j, 0)
        table = jnp.where(j >= 0, n_back - t, n_back + 1)
        rows = pl.ds(pl.multiple_of(jc * BLK, BLK), BLK)
        s2 = (SCALE * LOG2E) * _nt(k_ref[rows, :], q) + bias_ref[table]
        s_ref[t] = s2
        bmax = jnp.max(s2, axis=0, keepdims=True)
        m2 = bmax if m2 is None else jnp.maximum(m2, bmax)
        v_blocks.append(jc)
    acc = None
    for t, jc in enumerate(v_blocks):
        p = jnp.exp2(s_ref[t] - m2)
        d = jnp.dot(vt_ref[jc], p.astype(BF16), preferred_element_type=F32)
        acc = d if acc is None else acc + d
    l = acc[HEAD_DIM:HEAD_DIM + 1, :]
    return acc[0:HEAD_DIM, :] / l, m2 * LN2 + jnp.log(l)


def _dilated_kernel(slopes_ref, *refs):
    q_refs, k_refs, vt_refs, o_ref = refs[0:3], refs[3:6], refs[6:9], refs[9]
    bias_refs, s_refs = refs[10:13], refs[13:16]
    hg = pl.program_id(0)
    qb = pl.program_id(1)

    @pl.when(qb == 0)
    def _():
        key_i = lax.broadcasted_iota(jnp.int32, (BLK, BLK), 0)
        qry_i = lax.broadcasted_iota(jnp.int32, (BLK, BLK), 1)
        for g, (window, dilation) in enumerate(DILATED_CONFIGS):
            slope = slopes_ref[g * HEADS_PER_GROUP + hg]
            n_back = _blocks_back(window)
            for delta in range(n_back + 1):
                dist = delta * BLK + qry_i - key_i
                ok = (dist >= 0) & (dist <= window) & ((dist & (dilation - 1)) == 0)
                bias_refs[g][delta] = jnp.where(ok, (-LOG2E * slope) * dist.astype(F32), NEG_INF)
            bias_refs[g][n_back + 1] = jnp.full((BLK, BLK), NEG_INF, F32)

    outs, lses = [], []
    for g, (window, dilation) in enumerate(DILATED_CONFIGS):
        o, lse = _dilated_group(q_refs[g][...], k_refs[g], vt_refs[g], bias_refs[g], s_refs[g], qb,
                                _blocks_back(window))
        outs.append(o)
        lses.append(lse)
    mx = functools.reduce(jnp.maximum, lses)
    es = [jnp.exp(lse - mx) for lse in lses]
    tot = functools.reduce(lambda a, b: a + b, es)
    for g in range(N_GROUPS):
        o_ref[g] = ((es[g] / tot) * outs[g]).T.astype(BF16)


def dilated_attention(q_heads, k_aug, v_t, slopes):
    hpg = HEADS_PER_GROUP

    def q_spec(g):
        return pl.BlockSpec((None, BLK, HEAD_DIM), lambda hg, i: (g * hpg + hg, i, 0))

    def k_spec(g):
        return pl.BlockSpec((None, SEQ, HEAD_DIM), lambda hg, i: (g * hpg + hg, 0, 0))

    def v_spec(g):
        return pl.BlockSpec((None, N_BLOCKS, VT_ROWS, BLK), lambda hg, i: (g * hpg + hg, 0, 0, 0))

    in_specs = ([pl.BlockSpec(memory_space=pltpu.SMEM)]
                + [q_spec(g) for g in range(N_GROUPS)]
                + [k_spec(g) for g in range(N_GROUPS)]
                + [v_spec(g) for g in range(N_GROUPS)])
    scratch = ([pltpu.VMEM((_blocks_back(w) + 2, BLK, BLK), F32) for w, _ in DILATED_CONFIGS]
               + [pltpu.VMEM((_blocks_back(w) + 1, BLK, BLK), F32) for w, _ in DILATED_CONFIGS])
    return pl.pallas_call(
        _dilated_kernel,
        grid=(hpg, N_BLOCKS),
        in_specs=in_specs,
        out_specs=pl.BlockSpec((N_GROUPS, BLK, HEAD_DIM), lambda hg, i: (0, i, hg)),
        out_shape=jax.ShapeDtypeStruct((N_GROUPS, SEQ, GROUP_WIDTH), BF16),
        scratch_shapes=scratch,
        compiler_params=_params(("arbitrary", "arbitrary")),
        name="dilated_attn",
    )(slopes, *([q_heads] * 3), *([k_aug] * 3), *([v_t] * 3))


def _mem_attn_kernel(q_ref, k_ref, v_ref, o_ref):
    s = _nt(q_ref[...], k_ref[...]) * SCALE
    m = jnp.max(s, axis=-1, keepdims=True)
    p = jnp.exp(s - m)
    l = jnp.sum(p, axis=-1, keepdims=True)
    acc = jnp.dot(p.astype(BF16), v_ref[...], preferred_element_type=F32)
    o_ref[...] = (acc / l).astype(BF16)


def mem_attention(q_heads, mem_kv, tq=1024):
    return pl.pallas_call(
        _mem_attn_kernel,
        grid=(N_MEM_HEADS, SEQ // tq),
        in_specs=[pl.BlockSpec((None, tq, HEAD_DIM), lambda h, i: (N_SELF_HEADS + h, i, 0)),
                  pl.BlockSpec((None, N_MEM, HEAD_DIM), lambda h, i: (h, 0, 0)),
                  pl.BlockSpec((None, N_MEM, HEAD_DIM), lambda h, i: (N_MEM_HEADS + h, 0, 0))],
        out_specs=pl.BlockSpec((tq, HEAD_DIM), lambda h, i: (i, h)),
        out_shape=jax.ShapeDtypeStruct((SEQ, MEM_WIDTH), BF16),
        compiler_params=_params(("arbitrary", "arbitrary")),
        name="mem_attn",
    )(q_heads, mem_kv, mem_kv)


def kernel(x, mem, g_pre, g_post, g_mem, w_ffn_in, w_ffn_out, w_in, w_mem_kv, w_out):
    slopes = jnp.exp2(-8.0 * jnp.arange(1, N_SELF_HEADS + 1, dtype=F32) / N_SELF_HEADS)
    xs = x.reshape(SEQ, D_MODEL)
    mem2 = mem.reshape(N_MEM, D_MODEL)
    w_ffn_out = w_ffn_out.astype(BF16)
    w_out = w_out.astype(BF16)

    xn = rmsnorm_bf16(xs, g_pre[0, 0], tm=512)
    for layer in range(DEPTH):
        h = ffn_in(xn, w_ffn_in, layer, 0)
        xs, xn = ffn_out(h, w_ffn_out, layer, 0, xs, g_post[layer, 0], g_pre[layer, 1], 0.5)

        q_heads, k_aug, v_t = qkv_proj(xn, w_in, layer)
        mem_n = rmsnorm_bf16(mem2, g_mem[layer], tm=N_MEM)
        mem_kv = mem_kv_proj(mem_n, w_mem_kv, layer)
        if layer % 2 == 0:
            self_out = moba_attention(q_heads, k_aug, v_t, slopes)
        else:
            self_out = dilated_attention(q_heads, k_aug, v_t, slopes)
        mem_out = mem_attention(q_heads, mem_kv)
        xs, xn = out_proj(self_out, mem_out, w_out, layer, xs, g_post[layer, 1], g_pre[layer, 2])

        h = ffn_in(xn, w_ffn_in, layer, 1)
        g_next = g_pre[layer + 1, 0] if layer + 1 < DEPTH else g_pre[layer, 0]
        xs, xn = ffn_out(h, w_ffn_out, layer, 1, xs, g_post[layer, 2], g_next, 0.5)
    return xs.reshape(x.shape)
```

```python
import functools

import jax
import jax.numpy as jnp
from jax import lax
from jax.experimental import pallas as pl
from jax.experimental.pallas import tpu as pltpu

D_MODEL = 2048
SEQ = 8192
DEPTH = 4
HEAD_DIM = 128
N_SELF_HEADS = 12
N_MEM_HEADS = 4
SELF_WIDTH = N_SELF_HEADS * HEAD_DIM
MEM_WIDTH = N_MEM_HEADS * HEAD_DIM
IN_WIDTH = 3 * SELF_WIDTH + MEM_WIDTH
N_MEM = 256
D_FF = 5632
MOBA_BLOCK = 256
MOBA_TOP_K = 3
DILATED_CONFIGS = ((128, 1), (512, 4), (2048, 16))
N_GROUPS = len(DILATED_CONFIGS)
HEADS_PER_GROUP = N_SELF_HEADS // N_GROUPS
GROUP_WIDTH = HEADS_PER_GROUP * HEAD_DIM
RMS_EPS = 1e-6
NEG_INF = -1e30
SCALE = HEAD_DIM ** -0.5

BLK = MOBA_BLOCK
N_BLOCKS = SEQ // BLK
MOBA_HEADS_PER_STEP = 4
VT_ROWS = HEAD_DIM + 16
LOG2E = 1.4426950408889634
LN2 = 0.6931471805599453
KAUG_WIDTH = 2 * HEAD_DIM
ONEHOT_COL = HEAD_DIM
OFFSET_COL = HEAD_DIM + N_BLOCKS
FFN_ROW_CHUNK = 512
OUT_ROW_CHUNK = 128
PROJ_TN = 512
HEADS_PER_TILE = PROJ_TN // HEAD_DIM
V7X_VMEM_LIMIT = 56 * 1024 * 1024

F32 = jnp.float32
BF16 = jnp.bfloat16
NT_DIMS = (((1,), (1,)), ((), ()))


def _params(semantics):
    return pltpu.CompilerParams(dimension_semantics=semantics, vmem_limit_bytes=V7X_VMEM_LIMIT)


def _rms(x, g):
    return x * lax.rsqrt(jnp.mean(x * x, axis=-1, keepdims=True) + RMS_EPS) * g


def _nt(a, b):
    return lax.dot_general(a, b, NT_DIMS, preferred_element_type=F32)


def _rmsnorm_kernel(x_ref, g_ref, o_ref):
    o_ref[...] = _rms(x_ref[...], g_ref[...]).astype(BF16)


def rmsnorm_bf16(x, g, tm):
    m, d = x.shape
    return pl.pallas_call(
        _rmsnorm_kernel,
        grid=(m // tm,),
        in_specs=[pl.BlockSpec((tm, d), lambda i: (i, 0)),
                  pl.BlockSpec((1, d), lambda i: (0, 0))],
        out_specs=pl.BlockSpec((tm, d), lambda i: (i, 0)),
        out_shape=jax.ShapeDtypeStruct((m, d), BF16),
        compiler_params=_params(("arbitrary",)),
        name="rmsnorm",
    )(x, g.reshape(1, d))


def _ffn_in_kernel(x_ref, wg_ref, wu_ref, o_ref):
    wg = wg_ref[...].astype(BF16)
    wu = wu_ref[...].astype(BF16)
    for c in range(x_ref.shape[0] // FFN_ROW_CHUNK):
        rows = slice(c * FFN_ROW_CHUNK, (c + 1) * FFN_ROW_CHUNK)
        x = x_ref[rows, :]
        gate = jnp.dot(x, wg, preferred_element_type=F32)
        up = jnp.dot(x, wu, preferred_element_type=F32)
        o_ref[rows, :] = (gate * jax.nn.sigmoid(gate) * up).astype(BF16)


def ffn_in(xn, w_all, layer, which, tm=2048, tn=512):
    m, d = xn.shape
    nj = D_FF // tn
    return pl.pallas_call(
        _ffn_in_kernel,
        grid=(m // tm, nj),
        in_specs=[pl.BlockSpec((tm, d), lambda i, j: (i, 0)),
                  pl.BlockSpec((None, None, d, tn), lambda i, j: (layer, which, 0, j)),
                  pl.BlockSpec((None, None, d, tn), lambda i, j: (layer, which, 0, j + nj))],
        out_specs=pl.BlockSpec((tm, tn), lambda i, j: (i, j)),
        out_shape=jax.ShapeDtypeStruct((m, D_FF), BF16),
        compiler_params=_params(("arbitrary", "arbitrary")),
        name="ffn_in",
    )(xn, w_all, w_all)


def _residual_epilogue(f, rows, x_ref, gpost_ref, gnext_ref, xo_ref, xn_ref, coef):
    xnew = x_ref[rows, :] + coef * _rms(f, gpost_ref[...])
    xo_ref[rows, :] = xnew
    xn_ref[rows, :] = _rms(xnew, gnext_ref[...]).astype(BF16)


def _ffn_out_kernel(a_ref, w_ref, x_ref, gpost_ref, gnext_ref, xo_ref, xn_ref, *, coef):
    for c in range(a_ref.shape[0] // OUT_ROW_CHUNK):
        rows = slice(c * OUT_ROW_CHUNK, (c + 1) * OUT_ROW_CHUNK)
        f = jnp.dot(a_ref[rows, :], w_ref[...], preferred_element_type=F32)
        _residual_epilogue(f, rows, x_ref, gpost_ref, gnext_ref, xo_ref, xn_ref, coef)


def ffn_out(a, w_all, layer, which, x, g_post, g_next, coef, tm=256):
    m, k = a.shape
    d = w_all.shape[-1]
    row = lambda i: (i, 0)
    fixed = lambda i: (0, 0)
    return pl.pallas_call(
        functools.partial(_ffn_out_kernel, coef=coef),
        grid=(m // tm,),
        in_specs=[pl.BlockSpec((tm, k), row),
                  pl.BlockSpec((None, None, k, d), lambda i: (layer, which, 0, 0),
                               pipeline_mode=pl.Buffered(1)),
                  pl.BlockSpec((tm, d), row),
                  pl.BlockSpec((1, d), fixed),
                  pl.BlockSpec((1, d), fixed)],
        out_specs=[pl.BlockSpec((tm, d), row), pl.BlockSpec((tm, d), row)],
        out_shape=[jax.ShapeDtypeStruct((m, d), F32), jax.ShapeDtypeStruct((m, d), BF16)],
        compiler_params=_params(("arbitrary",)),
        name="ffn_out",
    )(a, w_all, x, g_post.reshape(1, d), g_next.reshape(1, d))


def _out_proj_kernel(s_ref, m_ref, w_ref, x_ref, gpost_ref, gnext_ref, xo_ref, xn_ref):
    chunk = 2 * OUT_ROW_CHUNK
    for c in range(m_ref.shape[0] // chunk):
        rows = slice(c * chunk, (c + 1) * chunk)
        cat = jnp.concatenate([s_ref[g, rows, :] for g in range(N_GROUPS)] + [m_ref[rows, :]], axis=-1)
        f = jnp.dot(cat, w_ref[...], preferred_element_type=F32)
        _residual_epilogue(f, rows, x_ref, gpost_ref, gnext_ref, xo_ref, xn_ref, 1.0)


def out_proj(self_out, mem_out, w_all, layer, x, g_post, g_next, tm=512):
    m = x.shape[0]
    d = w_all.shape[-1]
    row = lambda i: (i, 0)
    fixed = lambda i: (0, 0)
    return pl.pallas_call(
        _out_proj_kernel,
        grid=(m // tm,),
        in_specs=[pl.BlockSpec((N_GROUPS, tm, GROUP_WIDTH), lambda i: (0, i, 0)),
                  pl.BlockSpec((tm, MEM_WIDTH), row),
                  pl.BlockSpec((None, SELF_WIDTH + MEM_WIDTH, d), lambda i: (layer, 0, 0),
                               pipeline_mode=pl.Buffered(1)),
                  pl.BlockSpec((tm, d), row),
                  pl.BlockSpec((1, d), fixed),
                  pl.BlockSpec((1, d), fixed)],
        out_specs=[pl.BlockSpec((tm, d), row), pl.BlockSpec((tm, d), row)],
        out_shape=[jax.ShapeDtypeStruct((m, d), F32), jax.ShapeDtypeStruct((m, d), BF16)],
        compiler_params=_params(("arbitrary",)),
        name="out_proj",
    )(self_out, mem_out, w_all, x, g_post.reshape(1, d), g_next.reshape(1, d))


Q_TILES = SELF_WIDTH // PROJ_TN
K_TILE0 = Q_TILES
V_TILE0 = 2 * Q_TILES
MQ_TILE = 3 * Q_TILES
N_PROJ_TILES = IN_WIDTH // PROJ_TN


def _qkv_proj_kernel(x_ref, w_ref, q_ref, k_ref, vt_ref):
    i = pl.program_id(0)
    j = pl.program_id(1)
    tm = x_ref.shape[0]
    w = w_ref[...].astype(BF16)

    def row_chunks():
        for c in range(tm // FFN_ROW_CHUNK):
            rows = slice(c * FFN_ROW_CHUNK, (c + 1) * FFN_ROW_CHUNK)
            yield c, rows, jnp.dot(x_ref[rows, :], w, preferred_element_type=F32)

    def head_cols(r, h):
        return r[:, h * HEAD_DIM:(h + 1) * HEAD_DIM]

    @pl.when((j < K_TILE0) | (j == MQ_TILE))
    def _():
        for _, rows, r in row_chunks():
            for h in range(HEADS_PER_TILE):
                q_ref[h, rows, :] = head_cols(r, h).astype(BF16)

    @pl.when((j >= K_TILE0) & (j < V_TILE0))
    def _():
        lane = lax.broadcasted_iota(jnp.int32, (FFN_ROW_CHUNK, HEAD_DIM), 1)
        is_offset = (lane >= OFFSET_COL - HEAD_DIM) & (lane < OFFSET_COL - HEAD_DIM + 3)
        for c, rows, r in row_chunks():
            pos = (i * tm + c * FFN_ROW_CHUNK
                   + lax.broadcasted_iota(jnp.int32, (FFN_ROW_CHUNK, HEAD_DIM), 0))
            blk_of_pos = lax.shift_right_logical(pos, BLK.bit_length() - 1)
            aug = jnp.where(lane == blk_of_pos, 1.0,
                            jnp.where(is_offset, (pos & (BLK - 1)).astype(F32), 0.0)).astype(BF16)
            for h in range(HEADS_PER_TILE):
                k_ref[h, rows, 0:HEAD_DIM] = head_cols(r, h).astype(BF16)
                k_ref[h, rows, HEAD_DIM:KAUG_WIDTH] = aug

    @pl.when((j >= V_TILE0) & (j < MQ_TILE))
    def _():
        blocks_per_chunk = FFN_ROW_CHUNK // BLK
        for c, _, r in row_chunks():
            for h in range(HEADS_PER_TILE):
                for b in range(blocks_per_chunk):
                    blk = head_cols(r, h)[b * BLK:(b + 1) * BLK, :]
                    vt_ref[h, c * blocks_per_chunk + b, 0:HEAD_DIM, :] = blk.T.astype(BF16)
                    vt_ref[h, c * blocks_per_chunk + b, HEAD_DIM:VT_ROWS, :] = jnp.ones(
                        (VT_ROWS - HEAD_DIM, BLK), BF16)


def qkv_proj(xn, w_all, layer, tm=2048):
    m, d = xn.shape

    def q_map(i, j):
        return (jnp.where(j < K_TILE0, j, jnp.where(j < MQ_TILE, K_TILE0 - 1, K_TILE0)), i, 0)

    def k_map(i, j):
        return (jnp.clip(j - K_TILE0, 0, Q_TILES - 1), i, 0)

    def v_map(i, j):
        return (jnp.clip(j - V_TILE0, 0, Q_TILES - 1), i, 0, 0)

    hp = HEADS_PER_TILE
    return pl.pallas_call(
        _qkv_proj_kernel,
        grid=(m // tm, N_PROJ_TILES),
        in_specs=[pl.BlockSpec((tm, d), lambda i, j: (i, 0)),
                  pl.BlockSpec((None, d, PROJ_TN), lambda i, j: (layer, 0, j))],
        out_specs=[pl.BlockSpec((hp, tm, HEAD_DIM), q_map),
                   pl.BlockSpec((hp, tm, KAUG_WIDTH), k_map),
                   pl.BlockSpec((hp, tm // BLK, VT_ROWS, BLK), v_map)],
        out_shape=[jax.ShapeDtypeStruct((N_SELF_HEADS + N_MEM_HEADS, m, HEAD_DIM), BF16),
                   jax.ShapeDtypeStruct((N_SELF_HEADS, m, KAUG_WIDTH), BF16),
                   jax.ShapeDtypeStruct((N_SELF_HEADS, m // BLK, VT_ROWS, BLK), BF16)],
        compiler_params=_params(("arbitrary", "arbitrary")),
        name="qkv_proj",
    )(xn, w_all)


def _head_proj_kernel(x_ref, w_ref, o_ref):
    r = jnp.dot(x_ref[...], w_ref[...].astype(BF16), preferred_element_type=F32).astype(BF16)
    for c in range(o_ref.shape[0]):
        o_ref[c] = r[:, c * HEAD_DIM:(c + 1) * HEAD_DIM]


def mem_kv_proj(mem_n, w_all, layer):
    m, d = mem_n.shape
    n = w_all.shape[-1]
    return pl.pallas_call(
        _head_proj_kernel,
        grid=(n // PROJ_TN,),
        in_specs=[pl.BlockSpec((m, d), lambda j: (0, 0)),
                  pl.BlockSpec((None, d, PROJ_TN), lambda j: (layer, 0, j))],
        out_specs=pl.BlockSpec((HEADS_PER_TILE, m, HEAD_DIM), lambda j: (j, 0, 0)),
        out_shape=jax.ShapeDtypeStruct((n // HEAD_DIM, m, HEAD_DIM), BF16),
        compiler_params=_params(("arbitrary",)),
        name="mem_kv_proj",
    )(mem_n, w_all)


def _split3(x):
    hi = x.astype(BF16).astype(F32)
    mid = (x - hi).astype(BF16).astype(F32)
    lo = (x - hi - mid).astype(BF16).astype(F32)
    return hi, mid, lo


def _moba_augment(q, gate_t, slope, qb):
    blk_i = lax.broadcasted_iota(jnp.int32, (N_BLOCKS, BLK), 0)
    past = blk_i < qb
    gate_t = jnp.where(past, gate_t, NEG_INF)
    sel = jnp.zeros((N_BLOCKS, BLK), jnp.bool_)
    for _ in range(MOBA_TOP_K):
        best = jnp.max(gate_t, axis=0, keepdims=True)
        first = jnp.min(jnp.where(gate_t == best, blk_i, N_BLOCKS), axis=0, keepdims=True)
        chosen = blk_i == first
        sel = sel | chosen
        gate_t = jnp.where(chosen, -jnp.inf, gate_t)
    bias_t = jnp.where(sel & past, 0.0, NEG_INF / SCALE)
    hi, mid, lo = _split3(jnp.full((8, BLK), slope / SCALE, F32))
    sub = lax.broadcasted_iota(jnp.int32, (8, BLK), 0)
    parts = jnp.where(sub == 0, hi, jnp.where(sub == 1, mid, jnp.where(sub == 2, lo, 0.0)))
    pad = jnp.zeros((HEAD_DIM - N_BLOCKS - 8, BLK), F32)
    aug = jnp.concatenate([bias_t, parts, pad], axis=0).T
    return jnp.concatenate([q, aug.astype(BF16)], axis=1)


def _moba_own_softmax(raw, slope):
    key_i = lax.broadcasted_iota(jnp.int32, (BLK, BLK), 0)
    qry_i = lax.broadcasted_iota(jnp.int32, (BLK, BLK), 1)
    s = SCALE * raw + slope * key_i.astype(F32)
    s = jnp.where(key_i <= qry_i, s, NEG_INF)
    m = jnp.max(s, axis=0, keepdims=True)
    return m, jnp.exp(s - m).astype(BF16)


def _moba_qk(q_aug, k_ref, blk, raw_ref):
    rows = pl.ds(pl.multiple_of(blk * BLK, BLK), BLK)
    raw_ref[...] = _nt(k_ref[rows, :], q_aug)


def _moba_softmax(raw_ref, p_ref, blk, slope, qb, m):
    shift = slope * ((blk - qb) * BLK).astype(F32)
    m_new = jnp.maximum(m, SCALE * jnp.max(raw_ref[...], axis=0, keepdims=True) + shift)
    alpha = jnp.exp(m - m_new)
    p = jnp.exp2((SCALE * LOG2E) * raw_ref[...] - LOG2E * (m_new - shift))
    p_ref[...] = p.astype(BF16)
    return m_new, alpha


def _moba_pv(p_ref, vt_ref, blk, alpha, acc_ref):
    acc_ref[...] = alpha * acc_ref[...] + jnp.dot(vt_ref[blk], p_ref[...], preferred_element_type=F32)


def _moba_kernel(slopes_ref, q_ref, k_ref, vt_ref, o_ref, kmean_ref, raw_ref, p_ref, acc_ref):
    hp = pl.program_id(0)
    qb = pl.program_id(1)
    nh = MOBA_HEADS_PER_STEP
    last_blk = N_BLOCKS - 1

    @pl.when(qb == 0)
    def _():
        for hh in range(nh):
            def mean_block(j, carry):
                rows = pl.ds(pl.multiple_of(j * BLK, BLK), BLK)
                kmean_ref[hh, pl.ds(j, 1), :] = jnp.mean(
                    k_ref[hh, rows, 0:HEAD_DIM].astype(F32), axis=0, keepdims=True)
                return carry
            lax.fori_loop(0, N_BLOCKS, mean_block, 0)

    own_rows = pl.ds(pl.multiple_of(qb * BLK, BLK), BLK)
    slopes = [slopes_ref[hp * nh + hh] for hh in range(nh)]
    qs = [q_ref[hh] for hh in range(nh)]
    gates = [_nt(kmean_ref[hh].astype(BF16), qs[hh]) for hh in range(nh)]
    own_raws = [_nt(k_ref[hh, own_rows, 0:HEAD_DIM], qs[hh]) for hh in range(nh)]
    q_augs = [_moba_augment(qs[hh], gates[hh], slopes[hh], qb) for hh in range(nh)]
    owns = [_moba_own_softmax(own_raws[hh], slopes[hh]) for hh in range(nh)]
    for hh in range(nh):
        acc_ref[hh] = jnp.dot(vt_ref[hh, qb], owns[hh][1], preferred_element_type=F32)
        _moba_qk(q_augs[hh], k_ref.at[hh], 0, raw_ref.at[0, hh])
        p_ref[1, hh] = jnp.zeros((BLK, BLK), BF16)
    states = [owns[hh][0] for hh in range(nh)] + [jnp.ones((1, BLK), F32) for _ in range(nh)]

    def body(c, carry):
        ms, alphas_odd = list(carry[:nh]), list(carry[nh:])
        even, odd = 2 * c, 2 * c + 1
        for hh in range(nh):
            k_h, vt_h, acc_h = k_ref.at[hh], vt_ref.at[hh], acc_ref.at[hh]
            _moba_pv(p_ref.at[1, hh], vt_h, jnp.maximum(even - 1, 0), alphas_odd[hh], acc_h)
            ms[hh], alpha_even = _moba_softmax(raw_ref.at[0, hh], p_ref.at[0, hh], even, slopes[hh], qb, ms[hh])
            _moba_qk(q_augs[hh], k_h, odd, raw_ref.at[1, hh])
            _moba_pv(p_ref.at[0, hh], vt_h, even, alpha_even, acc_h)
            ms[hh], alphas_odd[hh] = _moba_softmax(raw_ref.at[1, hh], p_ref.at[1, hh], odd, slopes[hh], qb,
                                                   ms[hh])
            _moba_qk(q_augs[hh], k_h, jnp.minimum(even + 2, last_blk), raw_ref.at[0, hh])
        return tuple(ms + alphas_odd)

    n_pairs = (qb + 1) // 2
    states = lax.fori_loop(0, n_pairs, body, tuple(states))
    for hh in range(nh):
        _moba_pv(p_ref.at[1, hh], vt_ref.at[hh], jnp.maximum(2 * n_pairs - 1, 0), states[nh + hh],
                 acc_ref.at[hh])
    for hh in range(nh):
        out_t = acc_ref[hh, 0:HEAD_DIM, :] / acc_ref[hh, HEAD_DIM:HEAD_DIM + 1, :]
        o_ref[:, hh * HEAD_DIM:(hh + 1) * HEAD_DIM] = out_t.T.astype(BF16)


def moba_attention(q_heads, k_aug, v_t, slopes):
    nh = MOBA_HEADS_PER_STEP
    steps_per_group = HEADS_PER_GROUP // nh
    return pl.pallas_call(
        _moba_kernel,
        grid=(N_SELF_HEADS // nh, N_BLOCKS),
        in_specs=[pl.BlockSpec(memory_space=pltpu.SMEM),
                  pl.BlockSpec((nh, BLK, HEAD_DIM), lambda h, i: (h, i, 0)),
                  pl.BlockSpec((nh, SEQ, KAUG_WIDTH), lambda h, i: (h, 0, 0),
                               pipeline_mode=pl.Buffered(1)),
                  pl.BlockSpec((nh, N_BLOCKS, VT_ROWS, BLK), lambda h, i: (h, 0, 0, 0),
                               pipeline_mode=pl.Buffered(1))],
        out_specs=pl.BlockSpec((None, BLK, nh * HEAD_DIM),
                               lambda h, i: (h // steps_per_group, i, h % steps_per_group)),
        out_shape=jax.ShapeDtypeStruct((N_GROUPS, SEQ, GROUP_WIDTH), BF16),
        scratch_shapes=[pltpu.VMEM((nh, N_BLOCKS, HEAD_DIM), F32),
                        pltpu.VMEM((2, nh, BLK, BLK), F32),
                        pltpu.VMEM((2, nh, BLK, BLK), BF16),
                        pltpu.VMEM((nh, VT_ROWS, BLK), F32)],
        compiler_params=_params(("arbitrary", "arbitrary")),
        name="moba_attn",
    )(slopes, q_heads, k_aug, v_t)


def _blocks_back(window):
    return max(window // BLK, 1)


def _dilated_tiles():
    return [(g, t) for g, (window, _) in enumerate(DILATED_CONFIGS) for t in range(_blocks_back(window) + 1)]


def _dilated_kernel(slopes_ref, *refs):
    q_refs, k_refs, vt_refs, o_ref = refs[0:3], refs[3:6], refs[6:9], refs[9]
    bias_refs = refs[10:13]
    hg = pl.program_id(0)
    qb = pl.program_id(1)

    @pl.when(qb == 0)
    def _():
        key_i = lax.broadcasted_iota(jnp.int32, (BLK, BLK), 0)
        qry_i = lax.broadcasted_iota(jnp.int32, (BLK, BLK), 1)
        for g, (window, dilation) in enumerate(DILATED_CONFIGS):
            slope = slopes_ref[g * HEADS_PER_GROUP + hg]
            n_back = _blocks_back(window)
            for delta in range(n_back + 1):
                dist = delta * BLK + qry_i - key_i
                ok = (dist >= 0) & (dist <= window) & ((dist & (dilation - 1)) == 0)
                bias_refs[g][delta] = jnp.where(ok, (-LOG2E * slope) * dist.astype(F32), NEG_INF)
            bias_refs[g][n_back + 1] = jnp.full((BLK, BLK), NEG_INF, F32)

    qs = [q_refs[g][...] for g in range(N_GROUPS)]
    tiles = _dilated_tiles()

    def block_of(g, t):
        n_back = _blocks_back(DILATED_CONFIGS[g][0])
        j = qb - n_back + t
        table = jnp.where(j >= 0, n_back - t, n_back + 1)
        return jnp.maximum(j, 0), table

    def score(g, t):
        jc, table = block_of(g, t)
        rows = pl.ds(pl.multiple_of(jc * BLK, BLK), BLK)
        return (SCALE * LOG2E) * _nt(k_refs[g][rows, :], qs[g]) + bias_refs[g][table]

    def softmax(s2):
        bmax = jnp.max(s2, axis=0, keepdims=True)
        return bmax, jnp.exp2(s2 - bmax).astype(BF16)

    def pv(g, t, p):
        jc, _ = block_of(g, t)
        return jnp.dot(vt_refs[g][jc], p, preferred_element_type=F32)

    maxes = [[] for _ in range(N_GROUPS)]
    parts = [[] for _ in range(N_GROUPS)]
    s_next = score(*tiles[0])
    p_prev = None
    for i, (g, t) in enumerate(tiles):
        s_cur = s_next
        if i + 1 < len(tiles):
            s_next = score(*tiles[i + 1])
        bmax, p_cur = softmax(s_cur)
        maxes[g].append(bmax)
        if p_prev is not None:
            pg, pt = tiles[i - 1]
            parts[pg].append(pv(pg, pt, p_prev))
        p_prev = p_cur
    pg, pt = tiles[-1]
    parts[pg].append(pv(pg, pt, p_prev))

    outs, lses = [], []
    for g in range(N_GROUPS):
        m2 = functools.reduce(jnp.maximum, maxes[g])
        acc = None
        for bmax, part in zip(maxes[g], parts[g]):
            d = jnp.exp2(bmax - m2) * part
            acc = d if acc is None else acc + d
        l = acc[HEAD_DIM:HEAD_DIM + 1, :]
        outs.append(acc[0:HEAD_DIM, :] / l)
        lses.append(m2 * LN2 + jnp.log(l))
    mx = functools.reduce(jnp.maximum, lses)
    es = [jnp.exp(lse - mx) for lse in lses]
    tot = functools.reduce(lambda a, b: a + b, es)
    for g in range(N_GROUPS):
        o_ref[g] = ((es[g] / tot) * outs[g]).T.astype(BF16)


def dilated_attention(q_heads, k_aug, v_t, slopes):
    hpg = HEADS_PER_GROUP

    def q_spec(g):
        return pl.BlockSpec((None, BLK, HEAD_DIM), lambda hg, i: (g * hpg + hg, i, 0))

    def k_spec(g):
        return pl.BlockSpec((None, SEQ, HEAD_DIM), lambda hg, i: (g * hpg + hg, 0, 0))

    def v_spec(g):
        return pl.BlockSpec((None, N_BLOCKS, VT_ROWS, BLK), lambda hg, i: (g * hpg + hg, 0, 0, 0))

    in_specs = ([pl.BlockSpec(memory_space=pltpu.SMEM)]
                + [q_spec(g) for g in range(N_GROUPS)]
                + [k_spec(g) for g in range(N_GROUPS)]
                + [v_spec(g) for g in range(N_GROUPS)])
    scratch = [pltpu.VMEM((_blocks_back(w) + 2, BLK, BLK), F32) for w, _ in DILATED_CONFIGS]
    return pl.pallas_call(
        _dilated_kernel,
        grid=(hpg, N_BLOCKS),
        in_specs=in_specs,
        out_specs=pl.BlockSpec((N_GROUPS, BLK, HEAD_DIM), lambda hg, i: (0, i, hg)),
        out_shape=jax.ShapeDtypeStruct((N_GROUPS, SEQ, GROUP_WIDTH), BF16),
        scratch_shapes=scratch,
        compiler_params=_params(("arbitrary", "arbitrary")),
        name="dilated_attn",
    )(slopes, *([q_heads] * 3), *([k_aug] * 3), *([v_t] * 3))


def _mem_attn_kernel(q_ref, k_ref, v_ref, o_ref):
    s = _nt(q_ref[...], k_ref[...]) * SCALE
    m = jnp.max(s, axis=-1, keepdims=True)
    p = jnp.exp(s - m)
    l = jnp.sum(p, axis=-1, keepdims=True)
    acc = jnp.dot(p.astype(BF16), v_ref[...], preferred_element_type=F32)
    o_ref[...] = (acc / l).astype(BF16)


def mem_attention(q_heads, mem_kv, tq=1024):
    return pl.pallas_call(
        _mem_attn_kernel,
        grid=(N_MEM_HEADS, SEQ // tq),
        in_specs=[pl.BlockSpec((None, tq, HEAD_DIM), lambda h, i: (N_SELF_HEADS + h, i, 0)),
                  pl.BlockSpec((None, N_MEM, HEAD_DIM), lambda h, i: (h, 0, 0)),
                  pl.BlockSpec((None, N_MEM, HEAD_DIM), lambda h, i: (N_MEM_HEADS + h, 0, 0))],
        out_specs=pl.BlockSpec((tq, HEAD_DIM), lambda h, i: (i, h)),
        out_shape=jax.ShapeDtypeStruct((SEQ, MEM_WIDTH), BF16),
        compiler_params=_params(("arbitrary", "arbitrary")),
        name="mem_attn",
    )(q_heads, mem_kv, mem_kv)


def kernel(x, mem, g_pre, g_post, g_mem, w_ffn_in, w_ffn_out, w_in, w_mem_kv, w_out):
    slopes = jnp.exp2(-8.0 * jnp.arange(1, N_SELF_HEADS + 1, dtype=F32) / N_SELF_HEADS)
    xs = x.reshape(SEQ, D_MODEL)
    mem2 = mem.reshape(N_MEM, D_MODEL)
    w_ffn_out = w_ffn_out.astype(BF16)
    w_out = w_out.astype(BF16)

    xn = rmsnorm_bf16(xs, g_pre[0, 0], tm=512)
    for layer in range(DEPTH):
        h = ffn_in(xn, w_ffn_in, layer, 0)
        xs, xn = ffn_out(h, w_ffn_out, layer, 0, xs, g_post[layer, 0], g_pre[layer, 1], 0.5)

        q_heads, k_aug, v_t = qkv_proj(xn, w_in, layer)
        mem_n = rmsnorm_bf16(mem2, g_mem[layer], tm=N_MEM)
        mem_kv = mem_kv_proj(mem_n, w_mem_kv, layer)
        if layer % 2 == 0:
            self_out = moba_attention(q_heads, k_aug, v_t, slopes)
        else:
            self_out = dilated_attention(q_heads, k_aug, v_t, slopes)
        mem_out = mem_attention(q_heads, mem_kv)
        xs, xn = out_proj(self_out, mem_out, w_out, layer, xs, g_post[layer, 1], g_pre[layer, 2])

        h = ffn_in(xn, w_ffn_in, layer, 1)
        g_next = g_pre[layer + 1, 0] if layer + 1 < DEPTH else g_pre[layer, 0]
        xs, xn = ffn_out(h, w_ffn_out, layer, 1, xs, g_post[layer, 2], g_next, 0.5)
    return xs.reshape(x.shape)
```

```python
import functools

import jax
import jax.numpy as jnp
from jax import lax
from jax.experimental import pallas as pl
from jax.experimental.pallas import tpu as pltpu

D_MODEL = 2048
SEQ = 8192
DEPTH = 4
HEAD_DIM = 128
N_SELF_HEADS = 12
N_MEM_HEADS = 4
SELF_WIDTH = N_SELF_HEADS * HEAD_DIM
MEM_WIDTH = N_MEM_HEADS * HEAD_DIM
IN_WIDTH = 3 * SELF_WIDTH + MEM_WIDTH
N_MEM = 256
D_FF = 5632
MOBA_BLOCK = 256
MOBA_TOP_K = 3
DILATED_CONFIGS = ((128, 1), (512, 4), (2048, 16))
N_GROUPS = len(DILATED_CONFIGS)
HEADS_PER_GROUP = N_SELF_HEADS // N_GROUPS
GROUP_WIDTH = HEADS_PER_GROUP * HEAD_DIM
RMS_EPS = 1e-6
NEG_INF = -1e30
SCALE = HEAD_DIM ** -0.5

BLK = MOBA_BLOCK
N_BLOCKS = SEQ // BLK
MOBA_HEADS_PER_STEP = 4
VT_ROWS = HEAD_DIM + 16
LOG2E = 1.4426950408889634
LN2 = 0.6931471805599453
KAUG_WIDTH = 2 * HEAD_DIM
ONEHOT_COL = HEAD_DIM
OFFSET_COL = HEAD_DIM + N_BLOCKS
FFN_ROW_CHUNK = 512
PROJ_TN = 512
HEADS_PER_TILE = PROJ_TN // HEAD_DIM
V7X_VMEM_LIMIT = 56 * 1024 * 1024

F32 = jnp.float32
BF16 = jnp.bfloat16
NT_DIMS = (((1,), (1,)), ((), ()))


def _params(semantics):
    return pltpu.CompilerParams(dimension_semantics=semantics, vmem_limit_bytes=V7X_VMEM_LIMIT)


def _rms(x, g):
    return x * lax.rsqrt(jnp.mean(x * x, axis=-1, keepdims=True) + RMS_EPS) * g


def _nt(a, b):
    return lax.dot_general(a, b, NT_DIMS, preferred_element_type=F32)


def _rmsnorm_kernel(x_ref, g_ref, o_ref):
    o_ref[...] = _rms(x_ref[...], g_ref[...]).astype(BF16)


def rmsnorm_bf16(x, g, tm):
    m, d = x.shape
    return pl.pallas_call(
        _rmsnorm_kernel,
        grid=(m // tm,),
        in_specs=[pl.BlockSpec((tm, d), lambda i: (i, 0)),
                  pl.BlockSpec((1, d), lambda i: (0, 0))],
        out_specs=pl.BlockSpec((tm, d), lambda i: (i, 0)),
        out_shape=jax.ShapeDtypeStruct((m, d), BF16),
        compiler_params=_params(("arbitrary",)),
        name="rmsnorm",
    )(x, g.reshape(1, d))


def _ffn_in_kernel(x_ref, wg_ref, wu_ref, wnext_ref, o_ref, wnext_bf16_ref):
    wg = wg_ref[...].astype(BF16)
    wu = wu_ref[...].astype(BF16)
    for c in range(x_ref.shape[0] // FFN_ROW_CHUNK):
        rows = slice(c * FFN_ROW_CHUNK, (c + 1) * FFN_ROW_CHUNK)
        x = x_ref[rows, :]
        gate = jnp.dot(x, wg, preferred_element_type=F32)
        up = jnp.dot(x, wu, preferred_element_type=F32)
        o_ref[rows, :] = (gate * jax.nn.sigmoid(gate) * up).astype(BF16)
    wnext_bf16_ref[...] = wnext_ref[...].astype(BF16)


def ffn_in(xn, w_all, w_next_all, layer, which, tm=2048, tn=512):
    m, d = xn.shape
    nj = D_FF // tn
    steps = (m // tm) * nj
    k_next, d_next = w_next_all.shape[-2:]
    slab = k_next // steps
    assert slab * steps == k_next and slab % 16 == 0
    return pl.pallas_call(
        _ffn_in_kernel,
        grid=(m // tm, nj),
        in_specs=[pl.BlockSpec((tm, d), lambda i, j: (i, 0)),
                  pl.BlockSpec((None, None, d, tn), lambda i, j: (layer, which, 0, j)),
                  pl.BlockSpec((None, None, d, tn), lambda i, j: (layer, which, 0, j + nj)),
                  pl.BlockSpec((None, None, slab, d_next), lambda i, j: (layer, which, i * nj + j, 0))],
        out_specs=[pl.BlockSpec((tm, tn), lambda i, j: (i, j)),
                   pl.BlockSpec((slab, d_next), lambda i, j: (i * nj + j, 0))],
        out_shape=[jax.ShapeDtypeStruct((m, D_FF), BF16),
                   jax.ShapeDtypeStruct((k_next, d_next), BF16)],
        compiler_params=_params(("arbitrary", "arbitrary")),
        name="ffn_in",
    )(xn, w_all, w_all, w_next_all)


OUT_ROW_CHUNK = 128


def _residual_epilogue(f, rows, x_ref, gpost_ref, gnext_ref, xo_ref, xn_ref, coef):
    xnew = x_ref[rows, :] + coef * _rms(f, gpost_ref[...])
    xo_ref[rows, :] = xnew
    xn_ref[rows, :] = _rms(xnew, gnext_ref[...]).astype(BF16)


def _ffn_out_kernel(*refs, coef, with_cast):
    a_ref, w_ref, x_ref, gpost_ref, gnext_ref = refs[:5]
    if with_cast:
        wnext_ref, xo_ref, xn_ref, wnext_bf16_ref = refs[5:]
        wnext_bf16_ref[...] = wnext_ref[...].astype(BF16)
    else:
        xo_ref, xn_ref = refs[5:]
    for c in range(a_ref.shape[0] // OUT_ROW_CHUNK):
        rows = slice(c * OUT_ROW_CHUNK, (c + 1) * OUT_ROW_CHUNK)
        f = jnp.dot(a_ref[rows, :], w_ref[...], preferred_element_type=F32)
        _residual_epilogue(f, rows, x_ref, gpost_ref, gnext_ref, xo_ref, xn_ref, coef)


def ffn_out(a, w, x, g_post, g_next, coef, w_next_all=None, layer=None, tm=256):
    m, k = a.shape
    d = w.shape[-1]
    n_tiles = m // tm
    row = lambda i: (i, 0)
    fixed = lambda i: (0, 0)
    in_specs = [pl.BlockSpec((tm, k), row),
                pl.BlockSpec((k, d), fixed, pipeline_mode=pl.Buffered(1)),
                pl.BlockSpec((tm, d), row),
                pl.BlockSpec((1, d), fixed),
                pl.BlockSpec((1, d), fixed)]
    out_specs = [pl.BlockSpec((tm, d), row), pl.BlockSpec((tm, d), row)]
    out_shape = [jax.ShapeDtypeStruct((m, d), F32), jax.ShapeDtypeStruct((m, d), BF16)]
    operands = [a, w, x, g_post.reshape(1, d), g_next.reshape(1, d)]
    with_cast = w_next_all is not None
    if with_cast:
        k_next, d_next = w_next_all.shape[-2:]
        slab = k_next // n_tiles
        assert slab * n_tiles == k_next and slab % 16 == 0
        in_specs.append(pl.BlockSpec((None, slab, d_next), lambda i: (layer, i, 0)))
        out_specs.append(pl.BlockSpec((slab, d_next), row))
        out_shape.append(jax.ShapeDtypeStruct((k_next, d_next), BF16))
        operands.append(w_next_all)
    return pl.pallas_call(
        functools.partial(_ffn_out_kernel, coef=coef, with_cast=with_cast),
        grid=(n_tiles,),
        in_specs=in_specs,
        out_specs=out_specs,
        out_shape=out_shape,
        compiler_params=_params(("arbitrary",)),
        name="ffn_out",
    )(*operands)


def _out_proj_kernel(s_ref, m_ref, w_ref, x_ref, gpost_ref, gnext_ref, xo_ref, xn_ref):
    chunk = 2 * OUT_ROW_CHUNK
    for c in range(m_ref.shape[0] // chunk):
        rows = slice(c * chunk, (c + 1) * chunk)
        cat = jnp.concatenate([s_ref[g, rows, :] for g in range(N_GROUPS)] + [m_ref[rows, :]], axis=-1)
        f = jnp.dot(cat, w_ref[...], preferred_element_type=F32)
        _residual_epilogue(f, rows, x_ref, gpost_ref, gnext_ref, xo_ref, xn_ref, 1.0)


def out_proj(self_out, mem_out, w, x, g_post, g_next, tm=512):
    m = x.shape[0]
    d = w.shape[-1]
    row = lambda i: (i, 0)
    fixed = lambda i: (0, 0)
    return pl.pallas_call(
        _out_proj_kernel,
        grid=(m // tm,),
        in_specs=[pl.BlockSpec((N_GROUPS, tm, GROUP_WIDTH), lambda i: (0, i, 0)),
                  pl.BlockSpec((tm, MEM_WIDTH), row),
                  pl.BlockSpec((SELF_WIDTH + MEM_WIDTH, d), fixed, pipeline_mode=pl.Buffered(1)),
                  pl.BlockSpec((tm, d), row),
                  pl.BlockSpec((1, d), fixed),
                  pl.BlockSpec((1, d), fixed)],
        out_specs=[pl.BlockSpec((tm, d), row), pl.BlockSpec((tm, d), row)],
        out_shape=[jax.ShapeDtypeStruct((m, d), F32), jax.ShapeDtypeStruct((m, d), BF16)],
        compiler_params=_params(("arbitrary",)),
        name="out_proj",
    )(self_out, mem_out, w, x, g_post.reshape(1, d), g_next.reshape(1, d))


Q_TILES = SELF_WIDTH // PROJ_TN
K_TILE0 = Q_TILES
V_TILE0 = 2 * Q_TILES
MQ_TILE = 3 * Q_TILES
N_PROJ_TILES = IN_WIDTH // PROJ_TN


def _qkv_proj_kernel(x_ref, w_ref, q_ref, k_ref, vt_ref):
    i = pl.program_id(0)
    j = pl.program_id(1)
    tm = x_ref.shape[0]
    w = w_ref[...].astype(BF16)

    def row_chunks():
        for c in range(tm // FFN_ROW_CHUNK):
            rows = slice(c * FFN_ROW_CHUNK, (c + 1) * FFN_ROW_CHUNK)
            yield c, rows, jnp.dot(x_ref[rows, :], w, preferred_element_type=F32)

    def head_cols(r, h):
        return r[:, h * HEAD_DIM:(h + 1) * HEAD_DIM]

    @pl.when((j < K_TILE0) | (j == MQ_TILE))
    def _():
        for _, rows, r in row_chunks():
            for h in range(HEADS_PER_TILE):
                q_ref[h, rows, :] = head_cols(r, h).astype(BF16)

    @pl.when((j >= K_TILE0) & (j < V_TILE0))
    def _():
        lane = lax.broadcasted_iota(jnp.int32, (FFN_ROW_CHUNK, HEAD_DIM), 1)
        is_offset = (lane >= OFFSET_COL - HEAD_DIM) & (lane < OFFSET_COL - HEAD_DIM + 3)
        for c, rows, r in row_chunks():
            pos = (i * tm + c * FFN_ROW_CHUNK
                   + lax.broadcasted_iota(jnp.int32, (FFN_ROW_CHUNK, HEAD_DIM), 0))
            blk_of_pos = lax.shift_right_logical(pos, BLK.bit_length() - 1)
            aug = jnp.where(lane == blk_of_pos, 1.0,
                            jnp.where(is_offset, (pos & (BLK - 1)).astype(F32), 0.0)).astype(BF16)
            for h in range(HEADS_PER_TILE):
                k_ref[h, rows, 0:HEAD_DIM] = head_cols(r, h).astype(BF16)
                k_ref[h, rows, HEAD_DIM:KAUG_WIDTH] = aug

    @pl.when((j >= V_TILE0) & (j < MQ_TILE))
    def _():
        blocks_per_chunk = FFN_ROW_CHUNK // BLK
        for c, _, r in row_chunks():
            for h in range(HEADS_PER_TILE):
                for b in range(blocks_per_chunk):
                    blk = head_cols(r, h)[b * BLK:(b + 1) * BLK, :]
                    vt_ref[h, c * blocks_per_chunk + b, 0:HEAD_DIM, :] = blk.T.astype(BF16)
                    vt_ref[h, c * blocks_per_chunk + b, HEAD_DIM:VT_ROWS, :] = jnp.ones(
                        (VT_ROWS - HEAD_DIM, BLK), BF16)


def qkv_proj(xn, w_all, layer, tm=2048):
    m, d = xn.shape

    def q_map(i, j):
        return (jnp.where(j < K_TILE0, j, jnp.where(j < MQ_TILE, K_TILE0 - 1, K_TILE0)), i, 0)

    def k_map(i, j):
        return (jnp.clip(j - K_TILE0, 0, Q_TILES - 1), i, 0)

    def v_map(i, j):
        return (jnp.clip(j - V_TILE0, 0, Q_TILES - 1), i, 0, 0)

    hp = HEADS_PER_TILE
    return pl.pallas_call(
        _qkv_proj_kernel,
        grid=(m // tm, N_PROJ_TILES),
        in_specs=[pl.BlockSpec((tm, d), lambda i, j: (i, 0)),
                  pl.BlockSpec((None, d, PROJ_TN), lambda i, j: (layer, 0, j))],
        out_specs=[pl.BlockSpec((hp, tm, HEAD_DIM), q_map),
                   pl.BlockSpec((hp, tm, KAUG_WIDTH), k_map),
                   pl.BlockSpec((hp, tm // BLK, VT_ROWS, BLK), v_map)],
        out_shape=[jax.ShapeDtypeStruct((N_SELF_HEADS + N_MEM_HEADS, m, HEAD_DIM), BF16),
                   jax.ShapeDtypeStruct((N_SELF_HEADS, m, KAUG_WIDTH), BF16),
                   jax.ShapeDtypeStruct((N_SELF_HEADS, m // BLK, VT_ROWS, BLK), BF16)],
        compiler_params=_params(("arbitrary", "arbitrary")),
        name="qkv_proj",
    )(xn, w_all)


def _head_proj_kernel(x_ref, w_ref, o_ref):
    r = jnp.dot(x_ref[...], w_ref[...].astype(BF16), preferred_element_type=F32).astype(BF16)
    for c in range(o_ref.shape[0]):
        o_ref[c] = r[:, c * HEAD_DIM:(c + 1) * HEAD_DIM]


def mem_kv_proj(mem_n, w_all, layer):
    m, d = mem_n.shape
    n = w_all.shape[-1]
    return pl.pallas_call(
        _head_proj_kernel,
        grid=(n // PROJ_TN,),
        in_specs=[pl.BlockSpec((m, d), lambda j: (0, 0)),
                  pl.BlockSpec((None, d, PROJ_TN), lambda j: (layer, 0, j))],
        out_specs=pl.BlockSpec((HEADS_PER_TILE, m, HEAD_DIM), lambda j: (j, 0, 0)),
        out_shape=jax.ShapeDtypeStruct((n // HEAD_DIM, m, HEAD_DIM), BF16),
        compiler_params=_params(("arbitrary",)),
        name="mem_kv_proj",
    )(mem_n, w_all)


def _split3(x):
    hi = x.astype(BF16).astype(F32)
    mid = (x - hi).astype(BF16).astype(F32)
    lo = (x - hi - mid).astype(BF16).astype(F32)
    return hi, mid, lo


def _moba_augment(q, gate_t, slope, qb):
    blk_i = lax.broadcasted_iota(jnp.int32, (N_BLOCKS, BLK), 0)
    past = blk_i < qb
    gate_t = jnp.where(past, gate_t, NEG_INF)
    sel = jnp.zeros((N_BLOCKS, BLK), jnp.bool_)
    for _ in range(MOBA_TOP_K):
        best = jnp.max(gate_t, axis=0, keepdims=True)
        first = jnp.min(jnp.where(gate_t == best, blk_i, N_BLOCKS), axis=0, keepdims=True)
        chosen = blk_i == first
        sel = sel | chosen
        gate_t = jnp.where(chosen, -jnp.inf, gate_t)
    bias_t = jnp.where(sel & past, 0.0, NEG_INF / SCALE)
    hi, mid, lo = _split3(jnp.full((8, BLK), slope / SCALE, F32))
    sub = lax.broadcasted_iota(jnp.int32, (8, BLK), 0)
    parts = jnp.where(sub == 0, hi, jnp.where(sub == 1, mid, jnp.where(sub == 2, lo, 0.0)))
    pad = jnp.zeros((HEAD_DIM - N_BLOCKS - 8, BLK), F32)
    aug = jnp.concatenate([bias_t, parts, pad], axis=0).T
    return jnp.concatenate([q, aug.astype(BF16)], axis=1)


def _moba_own_softmax(raw, slope):
    key_i = lax.broadcasted_iota(jnp.int32, (BLK, BLK), 0)
    qry_i = lax.broadcasted_iota(jnp.int32, (BLK, BLK), 1)
    s = SCALE * raw + slope * key_i.astype(F32)
    s = jnp.where(key_i <= qry_i, s, NEG_INF)
    m = jnp.max(s, axis=0, keepdims=True)
    return m, jnp.exp(s - m).astype(BF16)


def _moba_qk(q_aug, k_ref, blk, raw_ref):
    rows = pl.ds(pl.multiple_of(blk * BLK, BLK), BLK)
    raw_ref[...] = _nt(k_ref[rows, :], q_aug)


def _moba_softmax(raw_ref, p_ref, blk, slope, qb, m):
    shift = slope * ((blk - qb) * BLK).astype(F32)
    m_new = jnp.maximum(m, SCALE * jnp.max(raw_ref[...], axis=0, keepdims=True) + shift)
    alpha = jnp.exp(m - m_new)
    p = jnp.exp2((SCALE * LOG2E) * raw_ref[...] - LOG2E * (m_new - shift))
    p_ref[...] = p.astype(BF16)
    return m_new, alpha


def _moba_pv(p_ref, vt_ref, blk, alpha, acc_ref):
    acc_ref[...] = alpha * acc_ref[...] + jnp.dot(vt_ref[blk], p_ref[...], preferred_element_type=F32)


def _moba_kernel(slopes_ref, q_ref, k_ref, vt_ref, o_ref, kmean_ref, raw_ref, p_ref, acc_ref):
    hp = pl.program_id(0)
    qb = pl.program_id(1)
    nh = MOBA_HEADS_PER_STEP
    last_blk = N_BLOCKS - 1

    @pl.when(qb == 0)
    def _():
        for hh in range(nh):
            def mean_block(j, carry):
                rows = pl.ds(pl.multiple_of(j * BLK, BLK), BLK)
                kmean_ref[hh, pl.ds(j, 1), :] = jnp.mean(
                    k_ref[hh, rows, 0:HEAD_DIM].astype(F32), axis=0, keepdims=True)
                return carry
            lax.fori_loop(0, N_BLOCKS, mean_block, 0)

    own_rows = pl.ds(pl.multiple_of(qb * BLK, BLK), BLK)
    slopes = [slopes_ref[hp * nh + hh] for hh in range(nh)]
    qs = [q_ref[hh] for hh in range(nh)]
    gates = [_nt(kmean_ref[hh].astype(BF16), qs[hh]) for hh in range(nh)]
    own_raws = [_nt(k_ref[hh, own_rows, 0:HEAD_DIM], qs[hh]) for hh in range(nh)]
    q_augs = [_moba_augment(qs[hh], gates[hh], slopes[hh], qb) for hh in range(nh)]
    owns = [_moba_own_softmax(own_raws[hh], slopes[hh]) for hh in range(nh)]
    for hh in range(nh):
        acc_ref[hh] = jnp.dot(vt_ref[hh, qb], owns[hh][1], preferred_element_type=F32)
        _moba_qk(q_augs[hh], k_ref.at[hh], 0, raw_ref.at[0, hh])
        p_ref[1, hh] = jnp.zeros((BLK, BLK), BF16)
    states = [owns[hh][0] for hh in range(nh)] + [jnp.ones((1, BLK), F32) for _ in range(nh)]

    def body(c, carry):
        ms, alphas_odd = list(carry[:nh]), list(carry[nh:])
        even, odd = 2 * c, 2 * c + 1
        for hh in range(nh):
            k_h, vt_h, acc_h = k_ref.at[hh], vt_ref.at[hh], acc_ref.at[hh]
            _moba_pv(p_ref.at[1, hh], vt_h, jnp.maximum(even - 1, 0), alphas_odd[hh], acc_h)
            ms[hh], alpha_even = _moba_softmax(raw_ref.at[0, hh], p_ref.at[0, hh], even, slopes[hh], qb, ms[hh])
            _moba_qk(q_augs[hh], k_h, odd, raw_ref.at[1, hh])
            _moba_pv(p_ref.at[0, hh], vt_h, even, alpha_even, acc_h)
            ms[hh], alphas_odd[hh] = _moba_softmax(raw_ref.at[1, hh], p_ref.at[1, hh], odd, slopes[hh], qb,
                                                   ms[hh])
            _moba_qk(q_augs[hh], k_h, jnp.minimum(even + 2, last_blk), raw_ref.at[0, hh])
        return tuple(ms + alphas_odd)

    n_pairs = (qb + 1) // 2
    states = lax.fori_loop(0, n_pairs, body, tuple(states))
    for hh in range(nh):
        _moba_pv(p_ref.at[1, hh], vt_ref.at[hh], jnp.maximum(2 * n_pairs - 1, 0), states[nh + hh],
                 acc_ref.at[hh])
    for hh in range(nh):
        out_t = acc_ref[hh, 0:HEAD_DIM, :] / acc_ref[hh, HEAD_DIM:HEAD_DIM + 1, :]
        o_ref[:, hh * HEAD_DIM:(hh + 1) * HEAD_DIM] = out_t.T.astype(BF16)


def moba_attention(q_heads, k_aug, v_t, slopes):
    nh = MOBA_HEADS_PER_STEP
    steps_per_group = HEADS_PER_GROUP // nh
    return pl.pallas_call(
        _moba_kernel,
        grid=(N_SELF_HEADS // nh, N_BLOCKS),
        in_specs=[pl.BlockSpec(memory_space=pltpu.SMEM),
                  pl.BlockSpec((nh, BLK, HEAD_DIM), lambda h, i: (h, i, 0)),
                  pl.BlockSpec((nh, SEQ, KAUG_WIDTH), lambda h, i: (h, 0, 0),
                               pipeline_mode=pl.Buffered(1)),
                  pl.BlockSpec((nh, N_BLOCKS, VT_ROWS, BLK), lambda h, i: (h, 0, 0, 0),
                               pipeline_mode=pl.Buffered(1))],
        out_specs=pl.BlockSpec((None, BLK, nh * HEAD_DIM),
                               lambda h, i: (h // steps_per_group, i, h % steps_per_group)),
        out_shape=jax.ShapeDtypeStruct((N_GROUPS, SEQ, GROUP_WIDTH), BF16),
        scratch_shapes=[pltpu.VMEM((nh, N_BLOCKS, HEAD_DIM), F32),
                        pltpu.VMEM((2, nh, BLK, BLK), F32),
                        pltpu.VMEM((2, nh, BLK, BLK), BF16),
                        pltpu.VMEM((nh, VT_ROWS, BLK), F32)],
        compiler_params=_params(("arbitrary", "arbitrary")),
        name="moba_attn",
    )(slopes, q_heads, k_aug, v_t)


def _blocks_back(window):
    return max(window // BLK, 1)


def _dilated_tiles():
    return [(g, t) for g, (window, _) in enumerate(DILATED_CONFIGS) for t in range(_blocks_back(window) + 1)]


def _dilated_kernel(slopes_ref, *refs):
    q_refs, k_refs, vt_refs, o_ref = refs[0:3], refs[3:6], refs[6:9], refs[9]
    bias_refs = refs[10:13]
    hg = pl.program_id(0)
    qb = pl.program_id(1)

    @pl.when(qb == 0)
    def _():
        key_i = lax.broadcasted_iota(jnp.int32, (BLK, BLK), 0)
        qry_i = lax.broadcasted_iota(jnp.int32, (BLK, BLK), 1)
        for g, (window, dilation) in enumerate(DILATED_CONFIGS):
            slope = slopes_ref[g * HEADS_PER_GROUP + hg]
            n_back = _blocks_back(window)
            for delta in range(n_back + 1):
                dist = delta * BLK + qry_i - key_i
                ok = (dist >= 0) & (dist <= window) & ((dist & (dilation - 1)) == 0)
                bias_refs[g][delta] = jnp.where(ok, (-LOG2E * slope) * dist.astype(F32), NEG_INF)
            bias_refs[g][n_back + 1] = jnp.full((BLK, BLK), NEG_INF, F32)

    qs = [q_refs[g][...] for g in range(N_GROUPS)]
    tiles = _dilated_tiles()

    def block_of(g, t):
        n_back = _blocks_back(DILATED_CONFIGS[g][0])
        j = qb - n_back + t
        table = jnp.where(j >= 0, n_back - t, n_back + 1)
        return jnp.maximum(j, 0), table

    def score(g, t):
        jc, table = block_of(g, t)
        rows = pl.ds(pl.multiple_of(jc * BLK, BLK), BLK)
        return (SCALE * LOG2E) * _nt(k_refs[g][rows, :], qs[g]) + bias_refs[g][table]

    def softmax(s2):
        bmax = jnp.max(s2, axis=0, keepdims=True)
        return bmax, jnp.exp2(s2 - bmax).astype(BF16)

    def pv(g, t, p):
        jc, _ = block_of(g, t)
        return jnp.dot(vt_refs[g][jc], p, preferred_element_type=F32)

    maxes = [[] for _ in range(N_GROUPS)]
    parts = [[] for _ in range(N_GROUPS)]
    s_next = score(*tiles[0])
    p_prev = None
    for i, (g, t) in enumerate(tiles):
        s_cur = s_next
        if i + 1 < len(tiles):
            s_next = score(*tiles[i + 1])
        bmax, p_cur = softmax(s_cur)
        maxes[g].append(bmax)
        if p_prev is not None:
            pg, pt = tiles[i - 1]
            parts[pg].append(pv(pg, pt, p_prev))
        p_prev = p_cur
    pg, pt = tiles[-1]
    parts[pg].append(pv(pg, pt, p_prev))

    outs, lses = [], []
    for g in range(N_GROUPS):
        m2 = functools.reduce(jnp.maximum, maxes[g])
        acc = None
        for bmax, part in zip(maxes[g], parts[g]):
            d = jnp.exp2(bmax - m2) * part
            acc = d if acc is None else acc + d
        l = acc[HEAD_DIM:HEAD_DIM + 1, :]
        outs.append(acc[0:HEAD_DIM, :] / l)
        lses.append(m2 * LN2 + jnp.log(l))
    mx = functools.reduce(jnp.maximum, lses)
    es = [jnp.exp(lse - mx) for lse in lses]
    tot = functools.reduce(lambda a, b: a + b, es)
    for g in range(N_GROUPS):
        o_ref[g] = ((es[g] / tot) * outs[g]).T.astype(BF16)


def dilated_attention(q_heads, k_aug, v_t, slopes):
    hpg = HEADS_PER_GROUP

    def q_spec(g):
        return pl.BlockSpec((None, BLK, HEAD_DIM), lambda hg, i: (g * hpg + hg, i, 0))

    def k_spec(g):
        return pl.BlockSpec((None, SEQ, HEAD_DIM), lambda hg, i: (g * hpg + hg, 0, 0))

    def v_spec(g):
        return pl.BlockSpec((None, N_BLOCKS, VT_ROWS, BLK), lambda hg, i: (g * hpg + hg, 0, 0, 0))

    in_specs = ([pl.BlockSpec(memory_space=pltpu.SMEM)]
                + [q_spec(g) for g in range(N_GROUPS)]
                + [k_spec(g) for g in range(N_GROUPS)]
                + [v_spec(g) for g in range(N_GROUPS)])
    scratch = [pltpu.VMEM((_blocks_back(w) + 2, BLK, BLK), F32) for w, _ in DILATED_CONFIGS]
    return pl.pallas_call(
        _dilated_kernel,
        grid=(hpg, N_BLOCKS),
        in_specs=in_specs,
        out_specs=pl.BlockSpec((N_GROUPS, BLK, HEAD_DIM), lambda hg, i: (0, i, hg)),
        out_shape=jax.ShapeDtypeStruct((N_GROUPS, SEQ, GROUP_WIDTH), BF16),
        scratch_shapes=scratch,
        compiler_params=_params(("arbitrary", "arbitrary")),
        name="dilated_attn",
    )(slopes, *([q_heads] * 3), *([k_aug] * 3), *([v_t] * 3))


def _mem_attn_kernel(q_ref, k_ref, v_ref, o_ref):
    s = _nt(q_ref[...], k_ref[...]) * SCALE
    m = jnp.max(s, axis=-1, keepdims=True)
    p = jnp.exp(s - m)
    l = jnp.sum(p, axis=-1, keepdims=True)
    acc = jnp.dot(p.astype(BF16), v_ref[...], preferred_element_type=F32)
    o_ref[...] = (acc / l).astype(BF16)


def mem_attention(q_heads, mem_kv, tq=1024):
    return pl.pallas_call(
        _mem_attn_kernel,
        grid=(N_MEM_HEADS, SEQ // tq),
        in_specs=[pl.BlockSpec((None, tq, HEAD_DIM), lambda h, i: (N_SELF_HEADS + h, i, 0)),
                  pl.BlockSpec((None, N_MEM, HEAD_DIM), lambda h, i: (h, 0, 0)),
                  pl.BlockSpec((None, N_MEM, HEAD_DIM), lambda h, i: (N_MEM_HEADS + h, 0, 0))],
        out_specs=pl.BlockSpec((tq, HEAD_DIM), lambda h, i: (i, h)),
        out_shape=jax.ShapeDtypeStruct((SEQ, MEM_WIDTH), BF16),
        compiler_params=_params(("arbitrary", "arbitrary")),
        name="mem_attn",
    )(q_heads, mem_kv, mem_kv)


def kernel(x, mem, g_pre, g_post, g_mem, w_ffn_in, w_ffn_out, w_in, w_mem_kv, w_out):
    slopes = jnp.exp2(-8.0 * jnp.arange(1, N_SELF_HEADS + 1, dtype=F32) / N_SELF_HEADS)
    xs = x.reshape(SEQ, D_MODEL)
    mem2 = mem.reshape(N_MEM, D_MODEL)

    xn = rmsnorm_bf16(xs, g_pre[0, 0], tm=512)
    for layer in range(DEPTH):
        h, w_down = ffn_in(xn, w_ffn_in, w_ffn_out, layer, 0)
        xs, xn, w_o = ffn_out(h, w_down, xs, g_post[layer, 0], g_pre[layer, 1], 0.5, w_out, layer)

        q_heads, k_aug, v_t = qkv_proj(xn, w_in, layer)
        mem_n = rmsnorm_bf16(mem2, g_mem[layer], tm=N_MEM)
        mem_kv = mem_kv_proj(mem_n, w_mem_kv, layer)
        if layer % 2 == 0:
            self_out = moba_attention(q_heads, k_aug, v_t, slopes)
        else:
            self_out = dilated_attention(q_heads, k_aug, v_t, slopes)
        mem_out = mem_attention(q_heads, mem_kv)
        xs, xn = out_proj(self_out, mem_out, w_o, xs, g_post[layer, 1], g_pre[layer, 2])

        h, w_down = ffn_in(xn, w_ffn_in, w_ffn_out, layer, 1)
        g_next = g_pre[layer + 1, 0] if layer + 1 < DEPTH else g_pre[layer, 0]
        xs, xn = ffn_out(h, w_down, xs, g_post[layer, 2], g_next, 0.5)
    return xs.reshape(x.shape)
```

```python
import functools

import jax
import jax.numpy as jnp
from jax import lax
from jax.experimental import pallas as pl
from jax.experimental.pallas import tpu as pltpu

D_MODEL = 2048
SEQ = 8192
DEPTH = 4
HEAD_DIM = 128
N_SELF_HEADS = 12
N_MEM_HEADS = 4
SELF_WIDTH = N_SELF_HEADS * HEAD_DIM
MEM_WIDTH = N_MEM_HEADS * HEAD_DIM
IN_WIDTH = 3 * SELF_WIDTH + MEM_WIDTH
N_MEM = 256
D_FF = 5632
MOBA_BLOCK = 256
MOBA_TOP_K = 3
DILATED_CONFIGS = ((128, 1), (512, 4), (2048, 16))
N_GROUPS = len(DILATED_CONFIGS)
HEADS_PER_GROUP = N_SELF_HEADS // N_GROUPS
GROUP_WIDTH = HEADS_PER_GROUP * HEAD_DIM
RMS_EPS = 1e-6
NEG_INF = -1e30
SCALE = HEAD_DIM ** -0.5

BLK = MOBA_BLOCK
N_BLOCKS = SEQ // BLK
MOBA_HEADS_PER_STEP = 4
VT_ROWS = HEAD_DIM + 16
LOG2E = 1.4426950408889634
LN2 = 0.6931471805599453
KAUG_WIDTH = 2 * HEAD_DIM
ONEHOT_COL = HEAD_DIM
OFFSET_COL = HEAD_DIM + N_BLOCKS
FFN_ROW_CHUNK = 512
PROJ_TN = 512
HEADS_PER_TILE = PROJ_TN // HEAD_DIM
V7X_VMEM_LIMIT = 56 * 1024 * 1024

F32 = jnp.float32
BF16 = jnp.bfloat16
NT_DIMS = (((1,), (1,)), ((), ()))


def _params(semantics):
    return pltpu.CompilerParams(dimension_semantics=semantics, vmem_limit_bytes=V7X_VMEM_LIMIT)


def _rms(x, g):
    return x * lax.rsqrt(jnp.mean(x * x, axis=-1, keepdims=True) + RMS_EPS) * g


def _nt(a, b):
    return lax.dot_general(a, b, NT_DIMS, preferred_element_type=F32)


def _transposed(x):
    return x.astype(F32).T.astype(BF16)


def _rmsnorm_kernel(x_ref, g_ref, o_ref):
    o_ref[...] = _rms(x_ref[...], g_ref[...]).astype(BF16)


def rmsnorm_bf16(x, g, tm):
    m, d = x.shape
    return pl.pallas_call(
        _rmsnorm_kernel,
        grid=(m // tm,),
        in_specs=[pl.BlockSpec((tm, d), lambda i: (i, 0)),
                  pl.BlockSpec((1, d), lambda i: (0, 0))],
        out_specs=pl.BlockSpec((tm, d), lambda i: (i, 0)),
        out_shape=jax.ShapeDtypeStruct((m, d), BF16),
        compiler_params=_params(("arbitrary",)),
        name="rmsnorm",
    )(x, g.reshape(1, d))


def _ffn_in_kernel(x_ref, wg_ref, wu_ref, wnext_ref, o_ref, wnext_bf16_ref):
    wg = wg_ref[...].astype(BF16)
    wu = wu_ref[...].astype(BF16)
    for c in range(x_ref.shape[0] // FFN_ROW_CHUNK):
        rows = slice(c * FFN_ROW_CHUNK, (c + 1) * FFN_ROW_CHUNK)
        x = x_ref[rows, :]
        gate = jnp.dot(x, wg, preferred_element_type=F32)
        up = jnp.dot(x, wu, preferred_element_type=F32)
        o_ref[rows, :] = (gate * jax.nn.sigmoid(gate) * up).astype(BF16)
    wnext_bf16_ref[...] = wnext_ref[...].astype(BF16)


def ffn_in(xn, w_all, w_next_all, layer, which, tm=2048, tn=512):
    m, d = xn.shape
    nj = D_FF // tn
    steps = (m // tm) * nj
    k_next, d_next = w_next_all.shape[-2:]
    slab = k_next // steps
    assert slab * steps == k_next and slab % 16 == 0
    return pl.pallas_call(
        _ffn_in_kernel,
        grid=(m // tm, nj),
        in_specs=[pl.BlockSpec((tm, d), lambda i, j: (i, 0)),
                  pl.BlockSpec((None, None, d, tn), lambda i, j: (layer, which, 0, j)),
                  pl.BlockSpec((None, None, d, tn), lambda i, j: (layer, which, 0, j + nj)),
                  pl.BlockSpec((None, None, slab, d_next), lambda i, j: (layer, which, i * nj + j, 0))],
        out_specs=[pl.BlockSpec((tm, tn), lambda i, j: (i, j)),
                   pl.BlockSpec((slab, d_next), lambda i, j: (i * nj + j, 0))],
        out_shape=[jax.ShapeDtypeStruct((m, D_FF), BF16),
                   jax.ShapeDtypeStruct((k_next, d_next), BF16)],
        compiler_params=_params(("arbitrary", "arbitrary")),
        name="ffn_in",
    )(xn, w_all, w_all, w_next_all)


OUT_ROW_CHUNK = 128


def _residual_epilogue(f, rows, x_ref, gpost_ref, gnext_ref, xo_ref, xn_ref, coef):
    xnew = x_ref[rows, :] + coef * _rms(f, gpost_ref[...])
    xo_ref[rows, :] = xnew
    xn_ref[rows, :] = _rms(xnew, gnext_ref[...]).astype(BF16)


def _ffn_out_kernel(*refs, coef, with_cast):
    a_ref, w_ref, x_ref, gpost_ref, gnext_ref = refs[:5]
    if with_cast:
        wnext_ref, xo_ref, xn_ref, wnext_bf16_ref = refs[5:]
        wnext_bf16_ref[...] = wnext_ref[...].astype(BF16)
    else:
        xo_ref, xn_ref = refs[5:]
    for c in range(a_ref.shape[0] // OUT_ROW_CHUNK):
        rows = slice(c * OUT_ROW_CHUNK, (c + 1) * OUT_ROW_CHUNK)
        f = jnp.dot(a_ref[rows, :], w_ref[...], preferred_element_type=F32)
        _residual_epilogue(f, rows, x_ref, gpost_ref, gnext_ref, xo_ref, xn_ref, coef)


def ffn_out(a, w, x, g_post, g_next, coef, w_next_all=None, layer=None, tm=256):
    m, k = a.shape
    d = w.shape[-1]
    n_tiles = m // tm
    row = lambda i: (i, 0)
    fixed = lambda i: (0, 0)
    in_specs = [pl.BlockSpec((tm, k), row),
                pl.BlockSpec((k, d), fixed, pipeline_mode=pl.Buffered(1)),
                pl.BlockSpec((tm, d), row),
                pl.BlockSpec((1, d), fixed),
                pl.BlockSpec((1, d), fixed)]
    out_specs = [pl.BlockSpec((tm, d), row), pl.BlockSpec((tm, d), row)]
    out_shape = [jax.ShapeDtypeStruct((m, d), F32), jax.ShapeDtypeStruct((m, d), BF16)]
    operands = [a, w, x, g_post.reshape(1, d), g_next.reshape(1, d)]
    with_cast = w_next_all is not None
    if with_cast:
        k_next, d_next = w_next_all.shape[-2:]
        slab = k_next // n_tiles
        assert slab * n_tiles == k_next and slab % 16 == 0
        in_specs.append(pl.BlockSpec((None, slab, d_next), lambda i: (layer, i, 0)))
        out_specs.append(pl.BlockSpec((slab, d_next), row))
        out_shape.append(jax.ShapeDtypeStruct((k_next, d_next), BF16))
        operands.append(w_next_all)
    return pl.pallas_call(
        functools.partial(_ffn_out_kernel, coef=coef, with_cast=with_cast),
        grid=(n_tiles,),
        in_specs=in_specs,
        out_specs=out_specs,
        out_shape=out_shape,
        compiler_params=_params(("arbitrary",)),
        name="ffn_out",
    )(*operands)


def _out_proj_kernel(s_ref, m_ref, w_ref, x_ref, gpost_ref, gnext_ref, xo_ref, xn_ref):
    chunk = OUT_ROW_CHUNK
    for c in range(m_ref.shape[0] // chunk):
        rows = slice(c * chunk, (c + 1) * chunk)
        cat = jnp.concatenate([s_ref[g, rows, :] for g in range(N_GROUPS)] + [m_ref[rows, :]], axis=-1)
        f = jnp.dot(cat, w_ref[...], preferred_element_type=F32)
        _residual_epilogue(f, rows, x_ref, gpost_ref, gnext_ref, xo_ref, xn_ref, 1.0)


def out_proj(self_out, mem_out, w, x, g_post, g_next, tm=512):
    m = x.shape[0]
    d = w.shape[-1]
    row = lambda i: (i, 0)
    fixed = lambda i: (0, 0)
    return pl.pallas_call(
        _out_proj_kernel,
        grid=(m // tm,),
        in_specs=[pl.BlockSpec((N_GROUPS, tm, GROUP_WIDTH), lambda i: (0, i, 0)),
                  pl.BlockSpec((tm, MEM_WIDTH), row),
                  pl.BlockSpec((SELF_WIDTH + MEM_WIDTH, d), fixed, pipeline_mode=pl.Buffered(1)),
                  pl.BlockSpec((tm, d), row),
                  pl.BlockSpec((1, d), fixed),
                  pl.BlockSpec((1, d), fixed)],
        out_specs=[pl.BlockSpec((tm, d), row), pl.BlockSpec((tm, d), row)],
        out_shape=[jax.ShapeDtypeStruct((m, d), F32), jax.ShapeDtypeStruct((m, d), BF16)],
        compiler_params=_params(("arbitrary",)),
        name="out_proj",
    )(self_out, mem_out, w, x, g_post.reshape(1, d), g_next.reshape(1, d))


Q_TILES = SELF_WIDTH // PROJ_TN
K_TILE0 = Q_TILES
V_TILE0 = 2 * Q_TILES
MQ_TILE = 3 * Q_TILES
N_PROJ_TILES = IN_WIDTH // PROJ_TN


def _qkv_proj_kernel(x_ref, w_ref, q_ref, k_ref, vt_ref):
    i = pl.program_id(0)
    j = pl.program_id(1)
    tm = x_ref.shape[0]
    w = w_ref[...].astype(BF16)

    def row_chunks():
        for c in range(tm // FFN_ROW_CHUNK):
            rows = slice(c * FFN_ROW_CHUNK, (c + 1) * FFN_ROW_CHUNK)
            yield c, rows, jnp.dot(x_ref[rows, :], w, preferred_element_type=F32)

    def head_cols(r, h):
        return r[:, h * HEAD_DIM:(h + 1) * HEAD_DIM]

    @pl.when((j < K_TILE0) | (j == MQ_TILE))
    def _():
        for _, rows, r in row_chunks():
            for h in range(HEADS_PER_TILE):
                q_ref[h, rows, :] = head_cols(r, h).astype(BF16)

    @pl.when((j >= K_TILE0) & (j < V_TILE0))
    def _():
        lane = lax.broadcasted_iota(jnp.int32, (FFN_ROW_CHUNK, HEAD_DIM), 1)
        is_offset = (lane >= OFFSET_COL - HEAD_DIM) & (lane < OFFSET_COL - HEAD_DIM + 3)
        for c, rows, r in row_chunks():
            pos = (i * tm + c * FFN_ROW_CHUNK
                   + lax.broadcasted_iota(jnp.int32, (FFN_ROW_CHUNK, HEAD_DIM), 0))
            blk_of_pos = lax.shift_right_logical(pos, BLK.bit_length() - 1)
            aug = jnp.where(lane == blk_of_pos, 1.0,
                            jnp.where(is_offset, (pos & (BLK - 1)).astype(F32), 0.0)).astype(BF16)
            for h in range(HEADS_PER_TILE):
                k_ref[h, rows, 0:HEAD_DIM] = head_cols(r, h).astype(BF16)
                k_ref[h, rows, HEAD_DIM:KAUG_WIDTH] = aug

    @pl.when((j >= V_TILE0) & (j < MQ_TILE))
    def _():
        blocks_per_chunk = FFN_ROW_CHUNK // BLK
        for c, _, r in row_chunks():
            for h in range(HEADS_PER_TILE):
                for b in range(blocks_per_chunk):
                    blk = head_cols(r, h)[b * BLK:(b + 1) * BLK, :]
                    vt_ref[h, c * blocks_per_chunk + b, 0:HEAD_DIM, :] = blk.T.astype(BF16)
                    vt_ref[h, c * blocks_per_chunk + b, HEAD_DIM:VT_ROWS, :] = jnp.ones(
                        (VT_ROWS - HEAD_DIM, BLK), BF16)


def qkv_proj(xn, w_all, layer, tm=2048):
    m, d = xn.shape

    def q_map(i, j):
        return (jnp.where(j < K_TILE0, j, jnp.where(j < MQ_TILE, K_TILE0 - 1, K_TILE0)), i, 0)

    def k_map(i, j):
        return (jnp.clip(j - K_TILE0, 0, Q_TILES - 1), i, 0)

    def v_map(i, j):
        return (jnp.clip(j - V_TILE0, 0, Q_TILES - 1), i, 0, 0)

    hp = HEADS_PER_TILE
    return pl.pallas_call(
        _qkv_proj_kernel,
        grid=(m // tm, N_PROJ_TILES),
        in_specs=[pl.BlockSpec((tm, d), lambda i, j: (i, 0)),
                  pl.BlockSpec((None, d, PROJ_TN), lambda i, j: (layer, 0, j))],
        out_specs=[pl.BlockSpec((hp, tm, HEAD_DIM), q_map),
                   pl.BlockSpec((hp, tm, KAUG_WIDTH), k_map),
                   pl.BlockSpec((hp, tm // BLK, VT_ROWS, BLK), v_map)],
        out_shape=[jax.ShapeDtypeStruct((N_SELF_HEADS + N_MEM_HEADS, m, HEAD_DIM), BF16),
                   jax.ShapeDtypeStruct((N_SELF_HEADS, m, KAUG_WIDTH), BF16),
                   jax.ShapeDtypeStruct((N_SELF_HEADS, m // BLK, VT_ROWS, BLK), BF16)],
        compiler_params=_params(("arbitrary", "arbitrary")),
        name="qkv_proj",
    )(xn, w_all)


def _head_proj_kernel(x_ref, w_ref, o_ref):
    r = jnp.dot(x_ref[...], w_ref[...].astype(BF16), preferred_element_type=F32).astype(BF16)
    for c in range(o_ref.shape[0]):
        o_ref[c] = r[:, c * HEAD_DIM:(c + 1) * HEAD_DIM]


def mem_kv_proj(mem_n, w_all, layer):
    m, d = mem_n.shape
    n = w_all.shape[-1]
    return pl.pallas_call(
        _head_proj_kernel,
        grid=(n // PROJ_TN,),
        in_specs=[pl.BlockSpec((m, d), lambda j: (0, 0)),
                  pl.BlockSpec((None, d, PROJ_TN), lambda j: (layer, 0, j))],
        out_specs=pl.BlockSpec((HEADS_PER_TILE, m, HEAD_DIM), lambda j: (j, 0, 0)),
        out_shape=jax.ShapeDtypeStruct((n // HEAD_DIM, m, HEAD_DIM), BF16),
        compiler_params=_params(("arbitrary",)),
        name="mem_kv_proj",
    )(mem_n, w_all)


def _split3(x):
    hi = x.astype(BF16).astype(F32)
    mid = (x - hi).astype(BF16).astype(F32)
    lo = (x - hi - mid).astype(BF16).astype(F32)
    return hi, mid, lo


def _moba_augment(q_t, gate_t, slope, qb):
    blk_i = lax.broadcasted_iota(jnp.int32, (N_BLOCKS, BLK), 0)
    past = blk_i < qb
    gate_t = jnp.where(past, gate_t, NEG_INF)
    sel = jnp.zeros((N_BLOCKS, BLK), jnp.bool_)
    for _ in range(MOBA_TOP_K):
        best = jnp.max(gate_t, axis=0, keepdims=True)
        first = jnp.min(jnp.where(gate_t == best, blk_i, N_BLOCKS), axis=0, keepdims=True)
        chosen = blk_i == first
        sel = sel | chosen
        gate_t = jnp.where(chosen, -jnp.inf, gate_t)
    bias_t = jnp.where(sel & past, 0.0, NEG_INF / SCALE)
    hi, mid, lo = _split3(jnp.full((8, BLK), slope / SCALE, F32))
    sub = lax.broadcasted_iota(jnp.int32, (8, BLK), 0)
    parts = jnp.where(sub == 0, hi, jnp.where(sub == 1, mid, jnp.where(sub == 2, lo, 0.0)))
    pad = jnp.zeros((HEAD_DIM - N_BLOCKS - 8, BLK), F32)
    aug_t = jnp.concatenate([bias_t, parts, pad], axis=0)
    return jnp.concatenate([q_t, aug_t.astype(BF16)], axis=0)


def _moba_own_softmax(raw, slope):
    key_i = lax.broadcasted_iota(jnp.int32, (BLK, BLK), 0)
    qry_i = lax.broadcasted_iota(jnp.int32, (BLK, BLK), 1)
    s = SCALE * raw + slope * key_i.astype(F32)
    s = jnp.where(key_i <= qry_i, s, NEG_INF)
    m = jnp.max(s, axis=0, keepdims=True)
    return m, jnp.exp(s - m).astype(BF16)


def _moba_qk(q_aug_t, k_ref, blk, raw_ref):
    rows = pl.ds(pl.multiple_of(blk * BLK, BLK), BLK)
    raw_ref[...] = jnp.dot(k_ref[rows, :], q_aug_t, preferred_element_type=F32)


def _moba_softmax(raw_ref, p_ref, blk, slope, qb, m):
    shift = slope * ((blk - qb) * BLK).astype(F32)
    m_new = jnp.maximum(m, SCALE * jnp.max(raw_ref[...], axis=0, keepdims=True) + shift)
    alpha = jnp.exp(m - m_new)
    p = jnp.exp2((SCALE * LOG2E) * raw_ref[...] - LOG2E * (m_new - shift))
    p_ref[...] = p.astype(BF16)
    return m_new, alpha


def _moba_pv(p_ref, vt_ref, blk, alpha, acc_ref):
    acc_ref[...] = alpha * acc_ref[...] + jnp.dot(vt_ref[blk], p_ref[...], preferred_element_type=F32)


def _moba_kernel(slopes_ref, q_ref, k_ref, vt_ref, o_ref, kmean_ref, raw_ref, p_ref, acc_ref):
    hp = pl.program_id(0)
    qb = pl.program_id(1)
    nh = MOBA_HEADS_PER_STEP
    last_blk = N_BLOCKS - 1

    @pl.when(qb == 0)
    def _():
        for hh in range(nh):
            def mean_block(j, carry):
                rows = pl.ds(pl.multiple_of(j * BLK, BLK), BLK)
                kmean_ref[hh, pl.ds(j, 1), :] = jnp.mean(
                    k_ref[hh, rows, 0:HEAD_DIM].astype(F32), axis=0, keepdims=True)
                return carry
            lax.fori_loop(0, N_BLOCKS, mean_block, 0)

    own_rows = pl.ds(pl.multiple_of(qb * BLK, BLK), BLK)
    slopes = [slopes_ref[hp * nh + hh] for hh in range(nh)]
    qs = [_transposed(q_ref[hh]) for hh in range(nh)]
    gates = [jnp.dot(kmean_ref[hh].astype(BF16), qs[hh], preferred_element_type=F32)
             for hh in range(nh)]
    own_raws = [jnp.dot(k_ref[hh, own_rows, 0:HEAD_DIM], qs[hh], preferred_element_type=F32)
                for hh in range(nh)]
    q_augs = [_moba_augment(qs[hh], gates[hh], slopes[hh], qb) for hh in range(nh)]
    owns = [_moba_own_softmax(own_raws[hh], slopes[hh]) for hh in range(nh)]
    for hh in range(nh):
        acc_ref[hh] = jnp.dot(vt_ref[hh, qb], owns[hh][1], preferred_element_type=F32)
        _moba_qk(q_augs[hh], k_ref.at[hh], 0, raw_ref.at[0, hh])
        p_ref[1, hh] = jnp.zeros((BLK, BLK), BF16)
    states = [owns[hh][0] for hh in range(nh)] + [jnp.ones((1, BLK), F32) for _ in range(nh)]

    def body(c, carry):
        ms, alphas_odd = list(carry[:nh]), list(carry[nh:])
        even, odd = 2 * c, 2 * c + 1
        for hh in range(nh):
            k_h, vt_h, acc_h = k_ref.at[hh], vt_ref.at[hh], acc_ref.at[hh]
            _moba_pv(p_ref.at[1, hh], vt_h, jnp.maximum(even - 1, 0), alphas_odd[hh], acc_h)
            ms[hh], alpha_even = _moba_softmax(raw_ref.at[0, hh], p_ref.at[0, hh], even, slopes[hh], qb, ms[hh])
            _moba_qk(q_augs[hh], k_h, odd, raw_ref.at[1, hh])
            _moba_pv(p_ref.at[0, hh], vt_h, even, alpha_even, acc_h)
            ms[hh], alphas_odd[hh] = _moba_softmax(raw_ref.at[1, hh], p_ref.at[1, hh], odd, slopes[hh], qb,
                                                   ms[hh])
            _moba_qk(q_augs[hh], k_h, jnp.minimum(even + 2, last_blk), raw_ref.at[0, hh])
        return tuple(ms + alphas_odd)

    n_pairs = (qb + 1) // 2
    states = lax.fori_loop(0, n_pairs, body, tuple(states))
    for hh in range(nh):
        _moba_pv(p_ref.at[1, hh], vt_ref.at[hh], jnp.maximum(2 * n_pairs - 1, 0), states[nh + hh],
                 acc_ref.at[hh])
    for hh in range(nh):
        out_t = acc_ref[hh, 0:HEAD_DIM, :] / acc_ref[hh, HEAD_DIM:HEAD_DIM + 1, :]
        o_ref[:, hh * HEAD_DIM:(hh + 1) * HEAD_DIM] = out_t.T.astype(BF16)


def moba_attention(q_heads, k_aug, v_t, slopes):
    nh = MOBA_HEADS_PER_STEP
    steps_per_group = HEADS_PER_GROUP // nh
    return pl.pallas_call(
        _moba_kernel,
        grid=(N_SELF_HEADS // nh, N_BLOCKS),
        in_specs=[pl.BlockSpec(memory_space=pltpu.SMEM),
                  pl.BlockSpec((nh, BLK, HEAD_DIM), lambda h, i: (h, i, 0)),
                  pl.BlockSpec((nh, SEQ, KAUG_WIDTH), lambda h, i: (h, 0, 0),
                               pipeline_mode=pl.Buffered(1)),
                  pl.BlockSpec((nh, N_BLOCKS, VT_ROWS, BLK), lambda h, i: (h, 0, 0, 0),
                               pipeline_mode=pl.Buffered(1))],
        out_specs=pl.BlockSpec((None, BLK, nh * HEAD_DIM),
                               lambda h, i: (h // steps_per_group, i, h % steps_per_group)),
        out_shape=jax.ShapeDtypeStruct((N_GROUPS, SEQ, GROUP_WIDTH), BF16),
        scratch_shapes=[pltpu.VMEM((nh, N_BLOCKS, HEAD_DIM), F32),
                        pltpu.VMEM((2, nh, BLK, BLK), F32),
                        pltpu.VMEM((2, nh, BLK, BLK), BF16),
                        pltpu.VMEM((nh, VT_ROWS, BLK), F32)],
        compiler_params=_params(("arbitrary", "arbitrary")),
        name="moba_attn",
    )(slopes, q_heads, k_aug, v_t)


def _blocks_back(window):
    return max(window // BLK, 1)


def _dilated_tiles():
    return [(g, t) for g, (window, _) in enumerate(DILATED_CONFIGS) for t in range(_blocks_back(window) + 1)]


def _dilated_kernel(slopes_ref, *refs):
    q_refs, k_refs, vt_refs, o_ref = refs[0:3], refs[3:6], refs[6:9], refs[9]
    bias_refs = refs[10:13]
    hg = pl.program_id(0)
    qb = pl.program_id(1)

    @pl.when(qb == 0)
    def _():
        key_i = lax.broadcasted_iota(jnp.int32, (BLK, BLK), 0)
        qry_i = lax.broadcasted_iota(jnp.int32, (BLK, BLK), 1)
        for g, (window, dilation) in enumerate(DILATED_CONFIGS):
            slope = slopes_ref[g * HEADS_PER_GROUP + hg]
            n_back = _blocks_back(window)
            for delta in range(n_back + 1):
                dist = delta * BLK + qry_i - key_i
                ok = (dist >= 0) & (dist <= window) & ((dist & (dilation - 1)) == 0)
                bias_refs[g][delta] = jnp.where(ok, (-LOG2E * slope) * dist.astype(F32), NEG_INF)
            bias_refs[g][n_back + 1] = jnp.full((BLK, BLK), NEG_INF, F32)

    qs = [q_refs[g][...] for g in range(N_GROUPS)]
    tiles = _dilated_tiles()

    def block_of(g, t):
        n_back = _blocks_back(DILATED_CONFIGS[g][0])
        j = qb - n_back + t
        table = jnp.where(j >= 0, n_back - t, n_back + 1)
        return jnp.maximum(j, 0), table

    def score(g, t):
        jc, table = block_of(g, t)
        rows = pl.ds(pl.multiple_of(jc * BLK, BLK), BLK)
        return (SCALE * LOG2E) * _nt(k_refs[g][rows, :], qs[g]) + bias_refs[g][table]

    def softmax(s2):
        bmax = jnp.max(s2, axis=0, keepdims=True)
        return bmax, jnp.exp2(s2 - bmax).astype(BF16)

    def pv(g, t, p):
        jc, _ = block_of(g, t)
        return jnp.dot(vt_refs[g][jc], p, preferred_element_type=F32)

    maxes = [[] for _ in range(N_GROUPS)]
    parts = [[] for _ in range(N_GROUPS)]
    s_next = score(*tiles[0])
    p_prev = None
    for i, (g, t) in enumerate(tiles):
        s_cur = s_next
        if i + 1 < len(tiles):
            s_next = score(*tiles[i + 1])
        bmax, p_cur = softmax(s_cur)
        maxes[g].append(bmax)
        if p_prev is not None:
            pg, pt = tiles[i - 1]
            parts[pg].append(pv(pg, pt, p_prev))
        p_prev = p_cur
    pg, pt = tiles[-1]
    parts[pg].append(pv(pg, pt, p_prev))

    outs, lses = [], []
    for g in range(N_GROUPS):
        m2 = functools.reduce(jnp.maximum, maxes[g])
        acc = None
        for bmax, part in zip(maxes[g], parts[g]):
            d = jnp.exp2(bmax - m2) * part
            acc = d if acc is None else acc + d
        l = acc[HEAD_DIM:HEAD_DIM + 1, :]
        outs.append(acc[0:HEAD_DIM, :] / l)
        lses.append(m2 * LN2 + jnp.log(l))
    mx = functools.reduce(jnp.maximum, lses)
    es = [jnp.exp(lse - mx) for lse in lses]
    tot = functools.reduce(lambda a, b: a + b, es)
    for g in range(N_GROUPS):
        o_ref[g] = ((es[g] / tot) * outs[g]).T.astype(BF16)


def dilated_attention(q_heads, k_aug, v_t, slopes):
    hpg = HEADS_PER_GROUP

    def q_spec(g):
        return pl.BlockSpec((None, BLK, HEAD_DIM), lambda hg, i: (g * hpg + hg, i, 0))

    def k_spec(g):
        return pl.BlockSpec((None, SEQ, HEAD_DIM), lambda hg, i: (g * hpg + hg, 0, 0))

    def v_spec(g):
        return pl.BlockSpec((None, N_BLOCKS, VT_ROWS, BLK), lambda hg, i: (g * hpg + hg, 0, 0, 0))

    in_specs = ([pl.BlockSpec(memory_space=pltpu.SMEM)]
                + [q_spec(g) for g in range(N_GROUPS)]
                + [k_spec(g) for g in range(N_GROUPS)]
                + [v_spec(g) for g in range(N_GROUPS)])
    scratch = [pltpu.VMEM((_blocks_back(w) + 2, BLK, BLK), F32) for w, _ in DILATED_CONFIGS]
    return pl.pallas_call(
        _dilated_kernel,
        grid=(hpg, N_BLOCKS),
        in_specs=in_specs,
        out_specs=pl.BlockSpec((N_GROUPS, BLK, HEAD_DIM), lambda hg, i: (0, i, hg)),
        out_shape=jax.ShapeDtypeStruct((N_GROUPS, SEQ, GROUP_WIDTH), BF16),
        scratch_shapes=scratch,
        compiler_params=_params(("arbitrary", "arbitrary")),
        name="dilated_attn",
    )(slopes, *([q_heads] * 3), *([k_aug] * 3), *([v_t] * 3))


def _mem_attn_kernel(q_ref, k_ref, v_ref, o_ref):
    s = _nt(q_ref[...], k_ref[...]) * SCALE
    m = jnp.max(s, axis=-1, keepdims=True)
    p = jnp.exp(s - m)
    l = jnp.sum(p, axis=-1, keepdims=True)
    acc = jnp.dot(p.astype(BF16), v_ref[...], preferred_element_type=F32)
    o_ref[...] = (acc / l).astype(BF16)


def mem_attention(q_heads, mem_kv, tq=1024):
    return pl.pallas_call(
        _mem_attn_kernel,
        grid=(N_MEM_HEADS, SEQ // tq),
        in_specs=[pl.BlockSpec((None, tq, HEAD_DIM), lambda h, i: (N_SELF_HEADS + h, i, 0)),
                  pl.BlockSpec((None, N_MEM, HEAD_DIM), lambda h, i: (h, 0, 0)),
                  pl.BlockSpec((None, N_MEM, HEAD_DIM), lambda h, i: (N_MEM_HEADS + h, 0, 0))],
        out_specs=pl.BlockSpec((tq, HEAD_DIM), lambda h, i: (i, h)),
        out_shape=jax.ShapeDtypeStruct((SEQ, MEM_WIDTH), BF16),
        compiler_params=_params(("arbitrary", "arbitrary")),
        name="mem_attn",
    )(q_heads, mem_kv, mem_kv)


def kernel(x, mem, g_pre, g_post, g_mem, w_ffn_in, w_ffn_out, w_in, w_mem_kv, w_out):
    slopes = jnp.exp2(-8.0 * jnp.arange(1, N_SELF_HEADS + 1, dtype=F32) / N_SELF_HEADS)
    xs = x.reshape(SEQ, D_MODEL)
    mem2 = mem.reshape(N_MEM, D_MODEL)

    xn = rmsnorm_bf16(xs, g_pre[0, 0], tm=512)
    for layer in range(DEPTH):
        h, w_down = ffn_in(xn, w_ffn_in, w_ffn_out, layer, 0)
        xs, xn, w_o = ffn_out(h, w_down, xs, g_post[layer, 0], g_pre[layer, 1], 0.5, w_out, layer)

        q_heads, k_aug, v_t = qkv_proj(xn, w_in, layer)
        mem_n = rmsnorm_bf16(mem2, g_mem[layer], tm=N_MEM)
        mem_kv = mem_kv_proj(mem_n, w_mem_kv, layer)
        if layer % 2 == 0:
            self_out = moba_attention(q_heads, k_aug, v_t, slopes)
        else:
            self_out = dilated_attention(q_heads, k_aug, v_t, slopes)
        mem_out = mem_attention(q_heads, mem_kv)
        xs, xn = out_proj(self_out, mem_out, w_o, xs, g_post[layer, 1], g_pre[layer, 2])

        h, w_down = ffn_in(xn, w_ffn_in, w_ffn_out, layer, 1)
        g_next = g_pre[layer + 1, 0] if layer + 1 < DEPTH else g_pre[layer, 0]
        xs, xn = ffn_out(h, w_down, xs, g_post[layer, 2], g_next, 0.5)
    return xs.reshape(x.shape)
```

```python
import functools

import jax
import jax.numpy as jnp
from jax import lax
from jax.experimental import pallas as pl
from jax.experimental.pallas import tpu as pltpu

D_MODEL = 2048
SEQ = 8192
DEPTH = 4
HEAD_DIM = 128
N_SELF_HEADS = 12
N_MEM_HEADS = 4
SELF_WIDTH = N_SELF_HEADS * HEAD_DIM
MEM_WIDTH = N_MEM_HEADS * HEAD_DIM
IN_WIDTH = 3 * SELF_WIDTH + MEM_WIDTH
N_MEM = 256
D_FF = 5632
MOBA_BLOCK = 256
MOBA_TOP_K = 3
DILATED_CONFIGS = ((128, 1), (512, 4), (2048, 16))
N_GROUPS = len(DILATED_CONFIGS)
HEADS_PER_GROUP = N_SELF_HEADS // N_GROUPS
GROUP_WIDTH = HEADS_PER_GROUP * HEAD_DIM
RMS_EPS = 1e-6
NEG_INF = -1e30
SCALE = HEAD_DIM ** -0.5

BLK = MOBA_BLOCK
N_BLOCKS = SEQ // BLK
MOBA_HEADS_PER_STEP = 4
VT_ROWS = HEAD_DIM + 16
LOG2E = 1.4426950408889634
LN2 = 0.6931471805599453
OFFSET_LANE = N_BLOCKS
FFN_ROW_CHUNK = 512
PROJ_TN = 512
HEADS_PER_TILE = PROJ_TN // HEAD_DIM
V7X_VMEM_LIMIT = 56 * 1024 * 1024

F32 = jnp.float32
BF16 = jnp.bfloat16
NT_DIMS = (((1,), (1,)), ((), ()))


def _params(semantics):
    return pltpu.CompilerParams(dimension_semantics=semantics, vmem_limit_bytes=V7X_VMEM_LIMIT)


def _rms(x, g):
    return x * lax.rsqrt(jnp.mean(x * x, axis=-1, keepdims=True) + RMS_EPS) * g


def _nt(a, b):
    return lax.dot_general(a, b, NT_DIMS, preferred_element_type=F32)


def _transposed(x):
    return x.astype(F32).T.astype(BF16)


def _rmsnorm_kernel(x_ref, g_ref, o_ref):
    o_ref[...] = _rms(x_ref[...], g_ref[...]).astype(BF16)


def rmsnorm_bf16(x, g, tm):
    m, d = x.shape
    return pl.pallas_call(
        _rmsnorm_kernel,
        grid=(m // tm,),
        in_specs=[pl.BlockSpec((tm, d), lambda i: (i, 0)),
                  pl.BlockSpec((1, d), lambda i: (0, 0))],
        out_specs=pl.BlockSpec((tm, d), lambda i: (i, 0)),
        out_shape=jax.ShapeDtypeStruct((m, d), BF16),
        compiler_params=_params(("arbitrary",)),
        name="rmsnorm",
    )(x, g.reshape(1, d))


def _ffn_in_kernel(x_ref, wg_ref, wu_ref, wnext_ref, o_ref, wnext_bf16_ref):
    wg = wg_ref[...].astype(BF16)
    wu = wu_ref[...].astype(BF16)
    for c in range(x_ref.shape[0] // FFN_ROW_CHUNK):
        rows = slice(c * FFN_ROW_CHUNK, (c + 1) * FFN_ROW_CHUNK)
        x = x_ref[rows, :]
        gate = jnp.dot(x, wg, preferred_element_type=F32)
        up = jnp.dot(x, wu, preferred_element_type=F32)
        o_ref[rows, :] = (gate * jax.nn.sigmoid(gate) * up).astype(BF16)
    wnext_bf16_ref[...] = wnext_ref[...].astype(BF16)


def ffn_in(xn, w_all, w_next_all, layer, which, tm=2048, tn=512):
    m, d = xn.shape
    nj = D_FF // tn
    steps = (m // tm) * nj
    k_next, d_next = w_next_all.shape[-2:]
    slab = k_next // steps
    assert slab * steps == k_next and slab % 16 == 0
    return pl.pallas_call(
        _ffn_in_kernel,
        grid=(m // tm, nj),
        in_specs=[pl.BlockSpec((tm, d), lambda i, j: (i, 0)),
                  pl.BlockSpec((None, None, d, tn), lambda i, j: (layer, which, 0, j)),
                  pl.BlockSpec((None, None, d, tn), lambda i, j: (layer, which, 0, j + nj)),
                  pl.BlockSpec((None, None, slab, d_next), lambda i, j: (layer, which, i * nj + j, 0))],
        out_specs=[pl.BlockSpec((tm, tn), lambda i, j: (i, j)),
                   pl.BlockSpec((slab, d_next), lambda i, j: (i * nj + j, 0))],
        out_shape=[jax.ShapeDtypeStruct((m, D_FF), BF16),
                   jax.ShapeDtypeStruct((k_next, d_next), BF16)],
        compiler_params=_params(("arbitrary", "arbitrary")),
        name="ffn_in",
    )(xn, w_all, w_all, w_next_all)


OUT_ROW_CHUNK = 128


def _residual_epilogue(f, rows, x_ref, gpost_ref, gnext_ref, xo_ref, xn_ref, coef):
    xnew = x_ref[rows, :] + coef * _rms(f, gpost_ref[...])
    xo_ref[rows, :] = xnew
    xn_ref[rows, :] = _rms(xnew, gnext_ref[...]).astype(BF16)


def _ffn_out_kernel(*refs, coef, with_cast):
    a_ref, w_ref, x_ref, gpost_ref, gnext_ref = refs[:5]
    if with_cast:
        wnext_ref, xo_ref, xn_ref, wnext_bf16_ref = refs[5:]
        wnext_bf16_ref[...] = wnext_ref[...].astype(BF16)
    else:
        xo_ref, xn_ref = refs[5:]
    for c in range(a_ref.shape[0] // OUT_ROW_CHUNK):
        rows = slice(c * OUT_ROW_CHUNK, (c + 1) * OUT_ROW_CHUNK)
        f = jnp.dot(a_ref[rows, :], w_ref[...], preferred_element_type=F32)
        _residual_epilogue(f, rows, x_ref, gpost_ref, gnext_ref, xo_ref, xn_ref, coef)


def ffn_out(a, w, x, g_post, g_next, coef, w_next_all=None, layer=None, tm=256):
    m, k = a.shape
    d = w.shape[-1]
    n_tiles = m // tm
    row = lambda i: (i, 0)
    fixed = lambda i: (0, 0)
    in_specs = [pl.BlockSpec((tm, k), row),
                pl.BlockSpec((k, d), fixed, pipeline_mode=pl.Buffered(1)),
                pl.BlockSpec((tm, d), row),
                pl.BlockSpec((1, d), fixed),
                pl.BlockSpec((1, d), fixed)]
    out_specs = [pl.BlockSpec((tm, d), row), pl.BlockSpec((tm, d), row)]
    out_shape = [jax.ShapeDtypeStruct((m, d), F32), jax.ShapeDtypeStruct((m, d), BF16)]
    operands = [a, w, x, g_post.reshape(1, d), g_next.reshape(1, d)]
    with_cast = w_next_all is not None
    if with_cast:
        k_next, d_next = w_next_all.shape[-2:]
        slab = k_next // n_tiles
        assert slab * n_tiles == k_next and slab % 16 == 0
        in_specs.append(pl.BlockSpec((None, slab, d_next), lambda i: (layer, i, 0)))
        out_specs.append(pl.BlockSpec((slab, d_next), row))
        out_shape.append(jax.ShapeDtypeStruct((k_next, d_next), BF16))
        operands.append(w_next_all)
    return pl.pallas_call(
        functools.partial(_ffn_out_kernel, coef=coef, with_cast=with_cast),
        grid=(n_tiles,),
        in_specs=in_specs,
        out_specs=out_specs,
        out_shape=out_shape,
        compiler_params=_params(("arbitrary",)),
        name="ffn_out",
    )(*operands)


def _out_proj_kernel(s_ref, m_ref, w_ref, x_ref, gpost_ref, gnext_ref, xo_ref, xn_ref):
    chunk = OUT_ROW_CHUNK
    for c in range(m_ref.shape[0] // chunk):
        rows = slice(c * chunk, (c + 1) * chunk)
        cat = jnp.concatenate([s_ref[g, rows, :] for g in range(N_GROUPS)] + [m_ref[rows, :]], axis=-1)
        f = jnp.dot(cat, w_ref[...], preferred_element_type=F32)
        _residual_epilogue(f, rows, x_ref, gpost_ref, gnext_ref, xo_ref, xn_ref, 1.0)


def out_proj(self_out, mem_out, w, x, g_post, g_next, tm=512):
    m = x.shape[0]
    d = w.shape[-1]
    row = lambda i: (i, 0)
    fixed = lambda i: (0, 0)
    return pl.pallas_call(
        _out_proj_kernel,
        grid=(m // tm,),
        in_specs=[pl.BlockSpec((N_GROUPS, tm, GROUP_WIDTH), lambda i: (0, i, 0)),
                  pl.BlockSpec((tm, MEM_WIDTH), row),
                  pl.BlockSpec((SELF_WIDTH + MEM_WIDTH, d), fixed, pipeline_mode=pl.Buffered(1)),
                  pl.BlockSpec((tm, d), row),
                  pl.BlockSpec((1, d), fixed),
                  pl.BlockSpec((1, d), fixed)],
        out_specs=[pl.BlockSpec((tm, d), row), pl.BlockSpec((tm, d), row)],
        out_shape=[jax.ShapeDtypeStruct((m, d), F32), jax.ShapeDtypeStruct((m, d), BF16)],
        compiler_params=_params(("arbitrary",)),
        name="out_proj",
    )(self_out, mem_out, w, x, g_post.reshape(1, d), g_next.reshape(1, d))


Q_TILES = SELF_WIDTH // PROJ_TN
K_TILE0 = Q_TILES
V_TILE0 = 2 * Q_TILES
MQ_TILE = 3 * Q_TILES
N_PROJ_TILES = IN_WIDTH // PROJ_TN


def _qkv_proj_kernel(x_ref, w_ref, q_ref, k_ref, vt_ref):
    j = pl.program_id(1)
    tm = x_ref.shape[0]
    w = w_ref[...].astype(BF16)

    def row_chunks():
        for c in range(tm // FFN_ROW_CHUNK):
            rows = slice(c * FFN_ROW_CHUNK, (c + 1) * FFN_ROW_CHUNK)
            yield c, rows, jnp.dot(x_ref[rows, :], w, preferred_element_type=F32)

    def head_cols(r, h):
        return r[:, h * HEAD_DIM:(h + 1) * HEAD_DIM]

    @pl.when((j < K_TILE0) | (j == MQ_TILE))
    def _():
        for _, rows, r in row_chunks():
            for h in range(HEADS_PER_TILE):
                q_ref[h, rows, :] = head_cols(r, h).astype(BF16)

    @pl.when((j >= K_TILE0) & (j < V_TILE0))
    def _():
        for _, rows, r in row_chunks():
            for h in range(HEADS_PER_TILE):
                k_ref[h, rows, :] = head_cols(r, h).astype(BF16)

    @pl.when((j >= V_TILE0) & (j < MQ_TILE))
    def _():
        blocks_per_chunk = FFN_ROW_CHUNK // BLK
        for c, _, r in row_chunks():
            for h in range(HEADS_PER_TILE):
                for b in range(blocks_per_chunk):
                    blk = head_cols(r, h)[b * BLK:(b + 1) * BLK, :]
                    vt_ref[h, c * blocks_per_chunk + b, 0:HEAD_DIM, :] = blk.T.astype(BF16)
                    vt_ref[h, c * blocks_per_chunk + b, HEAD_DIM:VT_ROWS, :] = jnp.ones(
                        (VT_ROWS - HEAD_DIM, BLK), BF16)


def qkv_proj(xn, w_all, layer, tm=2048):
    m, d = xn.shape

    def q_map(i, j):
        return (jnp.where(j < K_TILE0, j, jnp.where(j < MQ_TILE, K_TILE0 - 1, K_TILE0)), i, 0)

    def k_map(i, j):
        return (jnp.clip(j - K_TILE0, 0, Q_TILES - 1), i, 0)

    def v_map(i, j):
        return (jnp.clip(j - V_TILE0, 0, Q_TILES - 1), i, 0, 0)

    hp = HEADS_PER_TILE
    return pl.pallas_call(
        _qkv_proj_kernel,
        grid=(m // tm, N_PROJ_TILES),
        in_specs=[pl.BlockSpec((tm, d), lambda i, j: (i, 0)),
                  pl.BlockSpec((None, d, PROJ_TN), lambda i, j: (layer, 0, j))],
        out_specs=[pl.BlockSpec((hp, tm, HEAD_DIM), q_map),
                   pl.BlockSpec((hp, tm, HEAD_DIM), k_map),
                   pl.BlockSpec((hp, tm // BLK, VT_ROWS, BLK), v_map)],
        out_shape=[jax.ShapeDtypeStruct((N_SELF_HEADS + N_MEM_HEADS, m, HEAD_DIM), BF16),
                   jax.ShapeDtypeStruct((N_SELF_HEADS, m, HEAD_DIM), BF16),
                   jax.ShapeDtypeStruct((N_SELF_HEADS, m // BLK, VT_ROWS, BLK), BF16)],
        compiler_params=_params(("arbitrary", "arbitrary")),
        name="qkv_proj",
    )(xn, w_all)


def _head_proj_kernel(x_ref, w_ref, o_ref):
    r = jnp.dot(x_ref[...], w_ref[...].astype(BF16), preferred_element_type=F32).astype(BF16)
    for c in range(o_ref.shape[0]):
        o_ref[c] = r[:, c * HEAD_DIM:(c + 1) * HEAD_DIM]


def mem_kv_proj(mem_n, w_all, layer):
    m, d = mem_n.shape
    n = w_all.shape[-1]
    return pl.pallas_call(
        _head_proj_kernel,
        grid=(n // PROJ_TN,),
        in_specs=[pl.BlockSpec((m, d), lambda j: (0, 0)),
                  pl.BlockSpec((None, d, PROJ_TN), lambda j: (layer, 0, j))],
        out_specs=pl.BlockSpec((HEADS_PER_TILE, m, HEAD_DIM), lambda j: (j, 0, 0)),
        out_shape=jax.ShapeDtypeStruct((n // HEAD_DIM, m, HEAD_DIM), BF16),
        compiler_params=_params(("arbitrary",)),
        name="mem_kv_proj",
    )(mem_n, w_all)


def _split3(x):
    hi = x.astype(BF16).astype(F32)
    mid = (x - hi).astype(BF16).astype(F32)
    lo = (x - hi - mid).astype(BF16).astype(F32)
    return hi, mid, lo


def _moba_augment(q_t, gate_t, slope, qb):
    blk_i = lax.broadcasted_iota(jnp.int32, (N_BLOCKS, BLK), 0)
    past = blk_i < qb
    gate_t = jnp.where(past, gate_t, NEG_INF)
    sel = jnp.zeros((N_BLOCKS, BLK), jnp.bool_)
    for _ in range(MOBA_TOP_K):
        best = jnp.max(gate_t, axis=0, keepdims=True)
        first = jnp.min(jnp.where(gate_t == best, blk_i, N_BLOCKS), axis=0, keepdims=True)
        chosen = blk_i == first
        sel = sel | chosen
        gate_t = jnp.where(chosen, -jnp.inf, gate_t)
    bias_t = jnp.where(sel & past, 0.0, NEG_INF / SCALE)
    hi, mid, lo = _split3(jnp.full((8, BLK), slope / SCALE, F32))
    sub = lax.broadcasted_iota(jnp.int32, (8, BLK), 0)
    parts = jnp.where(sub == 0, hi, jnp.where(sub == 1, mid, jnp.where(sub == 2, lo, 0.0)))
    pad = jnp.zeros((HEAD_DIM - N_BLOCKS - 8, BLK), F32)
    aug_t = jnp.concatenate([bias_t, parts, pad], axis=0)
    return jnp.concatenate([q_t, aug_t.astype(BF16)], axis=0)


def _moba_own_softmax(raw, slope):
    key_i = lax.broadcasted_iota(jnp.int32, (BLK, BLK), 0)
    qry_i = lax.broadcasted_iota(jnp.int32, (BLK, BLK), 1)
    s = SCALE * raw + slope * key_i.astype(F32)
    s = jnp.where(key_i <= qry_i, s, NEG_INF)
    m = jnp.max(s, axis=0, keepdims=True)
    return m, jnp.exp(s - m).astype(BF16)


def _moba_qk(q_aug_t, k_ref, kaug_ref, blk, raw_ref):
    rows = pl.ds(pl.multiple_of(blk * BLK, BLK), BLK)
    k_aug = jnp.concatenate([k_ref[rows, :], kaug_ref[rows, :]], axis=1)
    raw_ref[...] = jnp.dot(k_aug, q_aug_t, preferred_element_type=F32)


def _moba_softmax(raw_ref, p_ref, blk, slope, qb, m):
    shift = slope * ((blk - qb) * BLK).astype(F32)
    m_new = jnp.maximum(m, SCALE * jnp.max(raw_ref[...], axis=0, keepdims=True) + shift)
    alpha = jnp.exp(m - m_new)
    p = jnp.exp2((SCALE * LOG2E) * raw_ref[...] - LOG2E * (m_new - shift))
    p_ref[...] = p.astype(BF16)
    return m_new, alpha


def _moba_pv(p_ref, vt_ref, blk, alpha, acc_ref):
    acc_ref[...] = alpha * acc_ref[...] + jnp.dot(vt_ref[blk], p_ref[...], preferred_element_type=F32)


def _moba_kernel(slopes_ref, q_ref, k_ref, kaug_ref, vt_ref, o_ref, kmean_ref, raw_ref, p_ref, acc_ref):
    hp = pl.program_id(0)
    qb = pl.program_id(1)
    nh = MOBA_HEADS_PER_STEP
    last_blk = N_BLOCKS - 1

    @pl.when(qb == 0)
    def _():
        for hh in range(nh):
            def mean_block(j, carry):
                rows = pl.ds(pl.multiple_of(j * BLK, BLK), BLK)
                kmean_ref[hh, pl.ds(j, 1), :] = jnp.mean(
                    k_ref[hh, rows, :].astype(F32), axis=0, keepdims=True)
                return carry
            lax.fori_loop(0, N_BLOCKS, mean_block, 0)

    own_rows = pl.ds(pl.multiple_of(qb * BLK, BLK), BLK)
    slopes = [slopes_ref[hp * nh + hh] for hh in range(nh)]
    qs = [_transposed(q_ref[hh]) for hh in range(nh)]
    gates = [jnp.dot(kmean_ref[hh].astype(BF16), qs[hh], preferred_element_type=F32)
             for hh in range(nh)]
    own_raws = [jnp.dot(k_ref[hh, own_rows, :], qs[hh], preferred_element_type=F32)
                for hh in range(nh)]
    q_augs = [_moba_augment(qs[hh], gates[hh], slopes[hh], qb) for hh in range(nh)]
    owns = [_moba_own_softmax(own_raws[hh], slopes[hh]) for hh in range(nh)]
    for hh in range(nh):
        acc_ref[hh] = jnp.dot(vt_ref[hh, qb], owns[hh][1], preferred_element_type=F32)
        _moba_qk(q_augs[hh], k_ref.at[hh], kaug_ref, 0, raw_ref.at[0, hh])
        p_ref[1, hh] = jnp.zeros((BLK, BLK), BF16)
    states = [owns[hh][0] for hh in range(nh)] + [jnp.ones((1, BLK), F32) for _ in range(nh)]

    def body(c, carry):
        ms, alphas_odd = list(carry[:nh]), list(carry[nh:])
        even, odd = 2 * c, 2 * c + 1
        for hh in range(nh):
            k_h, vt_h, acc_h = k_ref.at[hh], vt_ref.at[hh], acc_ref.at[hh]
            _moba_pv(p_ref.at[1, hh], vt_h, jnp.maximum(even - 1, 0), alphas_odd[hh], acc_h)
            ms[hh], alpha_even = _moba_softmax(raw_ref.at[0, hh], p_ref.at[0, hh], even, slopes[hh], qb, ms[hh])
            _moba_qk(q_augs[hh], k_h, kaug_ref, odd, raw_ref.at[1, hh])
            _moba_pv(p_ref.at[0, hh], vt_h, even, alpha_even, acc_h)
            ms[hh], alphas_odd[hh] = _moba_softmax(raw_ref.at[1, hh], p_ref.at[1, hh], odd, slopes[hh], qb,
                                                   ms[hh])
            _moba_qk(q_augs[hh], k_h, kaug_ref, jnp.minimum(even + 2, last_blk), raw_ref.at[0, hh])
        return tuple(ms + alphas_odd)

    n_pairs = (qb + 1) // 2
    states = lax.fori_loop(0, n_pairs, body, tuple(states))
    for hh in range(nh):
        _moba_pv(p_ref.at[1, hh], vt_ref.at[hh], jnp.maximum(2 * n_pairs - 1, 0), states[nh + hh],
                 acc_ref.at[hh])
    for hh in range(nh):
        out_t = acc_ref[hh, 0:HEAD_DIM, :] / acc_ref[hh, HEAD_DIM:HEAD_DIM + 1, :]
        o_ref[:, hh * HEAD_DIM:(hh + 1) * HEAD_DIM] = out_t.T.astype(BF16)


def moba_key_augmentation():
    pos = lax.broadcasted_iota(jnp.int32, (SEQ, HEAD_DIM), 0)
    lane = lax.broadcasted_iota(jnp.int32, (SEQ, HEAD_DIM), 1)
    is_offset = (lane >= OFFSET_LANE) & (lane < OFFSET_LANE + 3)
    table = jnp.where(lane == pos // BLK, 1.0, jnp.where(is_offset, (pos % BLK).astype(F32), 0.0))
    return table.astype(BF16)


def moba_attention(q_heads, keys, v_t, slopes):
    nh = MOBA_HEADS_PER_STEP
    steps_per_group = HEADS_PER_GROUP // nh
    return pl.pallas_call(
        _moba_kernel,
        grid=(N_SELF_HEADS // nh, N_BLOCKS),
        in_specs=[pl.BlockSpec(memory_space=pltpu.SMEM),
                  pl.BlockSpec((nh, BLK, HEAD_DIM), lambda h, i: (h, i, 0)),
                  pl.BlockSpec((nh, SEQ, HEAD_DIM), lambda h, i: (h, 0, 0)),
                  pl.BlockSpec((SEQ, HEAD_DIM), lambda h, i: (0, 0), pipeline_mode=pl.Buffered(1)),
                  pl.BlockSpec((nh, N_BLOCKS, VT_ROWS, BLK), lambda h, i: (h, 0, 0, 0))],
        out_specs=pl.BlockSpec((None, BLK, nh * HEAD_DIM),
                               lambda h, i: (h // steps_per_group, i, h % steps_per_group)),
        out_shape=jax.ShapeDtypeStruct((N_GROUPS, SEQ, GROUP_WIDTH), BF16),
        scratch_shapes=[pltpu.VMEM((nh, N_BLOCKS, HEAD_DIM), F32),
                        pltpu.VMEM((2, nh, BLK, BLK), F32),
                        pltpu.VMEM((2, nh, BLK, BLK), BF16),
                        pltpu.VMEM((nh, VT_ROWS, BLK), F32)],
        compiler_params=_params(("arbitrary", "arbitrary")),
        name="moba_attn",
    )(slopes, q_heads, keys, moba_key_augmentation(), v_t)


def _blocks_back(window):
    return max(window // BLK, 1)


def _dilated_tiles():
    return [(g, t) for g, (window, _) in enumerate(DILATED_CONFIGS) for t in range(_blocks_back(window) + 1)]


def _dilated_kernel(slopes_ref, *refs):
    q_refs, k_refs, vt_refs, o_ref = refs[0:3], refs[3:6], refs[6:9], refs[9]
    bias_refs = refs[10:13]
    hg = pl.program_id(0)
    qb = pl.program_id(1)

    @pl.when(qb == 0)
    def _():
        key_i = lax.broadcasted_iota(jnp.int32, (BLK, BLK), 0)
        qry_i = lax.broadcasted_iota(jnp.int32, (BLK, BLK), 1)
        for g, (window, dilation) in enumerate(DILATED_CONFIGS):
            slope = slopes_ref[g * HEADS_PER_GROUP + hg]
            n_back = _blocks_back(window)
            for delta in range(n_back + 1):
                dist = delta * BLK + qry_i - key_i
                ok = (dist >= 0) & (dist <= window) & ((dist & (dilation - 1)) == 0)
                bias_refs[g][delta] = jnp.where(ok, (-LOG2E * slope) * dist.astype(F32), NEG_INF)
            bias_refs[g][n_back + 1] = jnp.full((BLK, BLK), NEG_INF, F32)

    qs = [q_refs[g][...] for g in range(N_GROUPS)]
    tiles = _dilated_tiles()

    def block_of(g, t):
        n_back = _blocks_back(DILATED_CONFIGS[g][0])
        j = qb - n_back + t
        table = jnp.where(j >= 0, n_back - t, n_back + 1)
        return jnp.maximum(j, 0), table

    def score(g, t):
        jc, table = block_of(g, t)
        rows = pl.ds(pl.multiple_of(jc * BLK, BLK), BLK)
        return (SCALE * LOG2E) * _nt(k_refs[g][rows, :], qs[g]) + bias_refs[g][table]

    def softmax(s2):
        bmax = jnp.max(s2, axis=0, keepdims=True)
        return bmax, jnp.exp2(s2 - bmax).astype(BF16)

    def pv(g, t, p):
        jc, _ = block_of(g, t)
        return jnp.dot(vt_refs[g][jc], p, preferred_element_type=F32)

    maxes = [[] for _ in range(N_GROUPS)]
    parts = [[] for _ in range(N_GROUPS)]
    s_next = score(*tiles[0])
    p_prev = None
    for i, (g, t) in enumerate(tiles):
        s_cur = s_next
        if i + 1 < len(tiles):
            s_next = score(*tiles[i + 1])
        bmax, p_cur = softmax(s_cur)
        maxes[g].append(bmax)
        if p_prev is not None:
            pg, pt = tiles[i - 1]
            parts[pg].append(pv(pg, pt, p_prev))
        p_prev = p_cur
    pg, pt = tiles[-1]
    parts[pg].append(pv(pg, pt, p_prev))

    outs, lses = [], []
    for g in range(N_GROUPS):
        m2 = functools.reduce(jnp.maximum, maxes[g])
        acc = None
        for bmax, part in zip(maxes[g], parts[g]):
            d = jnp.exp2(bmax - m2) * part
            acc = d if acc is None else acc + d
        l = acc[HEAD_DIM:HEAD_DIM + 1, :]
        outs.append(acc[0:HEAD_DIM, :] / l)
        lses.append(m2 * LN2 + jnp.log(l))
    mx = functools.reduce(jnp.maximum, lses)
    es = [jnp.exp(lse - mx) for lse in lses]
    tot = functools.reduce(lambda a, b: a + b, es)
    for g in range(N_GROUPS):
        o_ref[g] = ((es[g] / tot) * outs[g]).T.astype(BF16)


def dilated_attention(q_heads, keys, v_t, slopes):
    hpg = HEADS_PER_GROUP

    def q_spec(g):
        return pl.BlockSpec((None, BLK, HEAD_DIM), lambda hg, i: (g * hpg + hg, i, 0))

    def k_spec(g):
        return pl.BlockSpec((None, SEQ, HEAD_DIM), lambda hg, i: (g * hpg + hg, 0, 0))

    def v_spec(g):
        return pl.BlockSpec((None, N_BLOCKS, VT_ROWS, BLK), lambda hg, i: (g * hpg + hg, 0, 0, 0))

    in_specs = ([pl.BlockSpec(memory_space=pltpu.SMEM)]
                + [q_spec(g) for g in range(N_GROUPS)]
                + [k_spec(g) for g in range(N_GROUPS)]
                + [v_spec(g) for g in range(N_GROUPS)])
    scratch = [pltpu.VMEM((_blocks_back(w) + 2, BLK, BLK), F32) for w, _ in DILATED_CONFIGS]
    return pl.pallas_call(
        _dilated_kernel,
        grid=(hpg, N_BLOCKS),
        in_specs=in_specs,
        out_specs=pl.BlockSpec((N_GROUPS, BLK, HEAD_DIM), lambda hg, i: (0, i, hg)),
        out_shape=jax.ShapeDtypeStruct((N_GROUPS, SEQ, GROUP_WIDTH), BF16),
        scratch_shapes=scratch,
        compiler_params=_params(("arbitrary", "arbitrary")),
        name="dilated_attn",
    )(slopes, *([q_heads] * 3), *([keys] * 3), *([v_t] * 3))


def _mem_attn_kernel(q_ref, k_ref, v_ref, o_ref):
    s = _nt(q_ref[...], k_ref[...]) * SCALE
    m = jnp.max(s, axis=-1, keepdims=True)
    p = jnp.exp(s - m)
    l = jnp.sum(p, axis=-1, keepdims=True)
    acc = jnp.dot(p.astype(BF16), v_ref[...], preferred_element_type=F32)
    o_ref[...] = (acc / l).astype(BF16)


def mem_attention(q_heads, mem_kv, tq=1024):
    return pl.pallas_call(
        _mem_attn_kernel,
        grid=(N_MEM_HEADS, SEQ // tq),
        in_specs=[pl.BlockSpec((None, tq, HEAD_DIM), lambda h, i: (N_SELF_HEADS + h, i, 0)),
                  pl.BlockSpec((None, N_MEM, HEAD_DIM), lambda h, i: (h, 0, 0)),
                  pl.BlockSpec((None, N_MEM, HEAD_DIM), lambda h, i: (N_MEM_HEADS + h, 0, 0))],
        out_specs=pl.BlockSpec((tq, HEAD_DIM), lambda h, i: (i, h)),
        out_shape=jax.ShapeDtypeStruct((SEQ, MEM_WIDTH), BF16),
        compiler_params=_params(("arbitrary", "arbitrary")),
        name="mem_attn",
    )(q_heads, mem_kv, mem_kv)


def kernel(x, mem, g_pre, g_post, g_mem, w_ffn_in, w_ffn_out, w_in, w_mem_kv, w_out):
    slopes = jnp.exp2(-8.0 * jnp.arange(1, N_SELF_HEADS + 1, dtype=F32) / N_SELF_HEADS)
    xs = x.reshape(SEQ, D_MODEL)
    mem2 = mem.reshape(N_MEM, D_MODEL)

    xn = rmsnorm_bf16(xs, g_pre[0, 0], tm=512)
    for layer in range(DEPTH):
        h, w_down = ffn_in(xn, w_ffn_in, w_ffn_out, layer, 0)
        xs, xn, w_o = ffn_out(h, w_down, xs, g_post[layer, 0], g_pre[layer, 1], 0.5, w_out, layer)

        q_heads, keys, v_t = qkv_proj(xn, w_in, layer)
        mem_n = rmsnorm_bf16(mem2, g_mem[layer], tm=N_MEM)
        mem_kv = mem_kv_proj(mem_n, w_mem_kv, layer)
        if layer % 2 == 0:
            self_out = moba_attention(q_heads, keys, v_t, slopes)
        else:
            self_out = dilated_attention(q_heads, keys, v_t, slopes)
        mem_out = mem_attention(q_heads, mem_kv)
        xs, xn = out_proj(self_out, mem_out, w_o, xs, g_post[layer, 1], g_pre[layer, 2])

        h, w_down = ffn_in(xn, w_ffn_in, w_ffn_out, layer, 1)
        g_next = g_pre[layer + 1, 0] if layer + 1 < DEPTH else g_pre[layer, 0]
        xs, xn = ffn_out(h, w_down, xs, g_post[layer, 2], g_next, 0.5)
    return xs.reshape(x.shape)
```

```python
import functools

import jax
import jax.numpy as jnp
from jax import lax
from jax.experimental import pallas as pl
from jax.experimental.pallas import tpu as pltpu

D_MODEL = 2048
SEQ = 8192
DEPTH = 4
HEAD_DIM = 128
N_SELF_HEADS = 12
N_MEM_HEADS = 4
SELF_WIDTH = N_SELF_HEADS * HEAD_DIM
MEM_WIDTH = N_MEM_HEADS * HEAD_DIM
IN_WIDTH = 3 * SELF_WIDTH + MEM_WIDTH
N_MEM = 256
D_FF = 5632
MOBA_BLOCK = 256
MOBA_TOP_K = 3
DILATED_CONFIGS = ((128, 1), (512, 4), (2048, 16))
N_GROUPS = len(DILATED_CONFIGS)
HEADS_PER_GROUP = N_SELF_HEADS // N_GROUPS
GROUP_WIDTH = HEADS_PER_GROUP * HEAD_DIM
RMS_EPS = 1e-6
NEG_INF = -1e30
SCALE = HEAD_DIM ** -0.5

BLK = MOBA_BLOCK
N_BLOCKS = SEQ // BLK
MOBA_HEADS_PER_STEP = 4
VT_ROWS = HEAD_DIM + 16
LOG2E = 1.4426950408889634
LN2 = 0.6931471805599453
OFFSET_LANE = N_BLOCKS
FFN_ROW_CHUNK = 512
PROJ_TN = 512
HEADS_PER_TILE = PROJ_TN // HEAD_DIM
V7X_VMEM_LIMIT = 56 * 1024 * 1024

F32 = jnp.float32
BF16 = jnp.bfloat16
NT_DIMS = (((1,), (1,)), ((), ()))


def _params(semantics):
    return pltpu.CompilerParams(dimension_semantics=semantics, vmem_limit_bytes=V7X_VMEM_LIMIT)


def _rms(x, g):
    return x * lax.rsqrt(jnp.mean(x * x, axis=-1, keepdims=True) + RMS_EPS) * g


def _nt(a, b):
    return lax.dot_general(a, b, NT_DIMS, preferred_element_type=F32)


def _transposed(x):
    return x.astype(F32).T.astype(BF16)


def _rmsnorm_kernel(x_ref, g_ref, o_ref):
    o_ref[...] = _rms(x_ref[...], g_ref[...]).astype(BF16)


def rmsnorm_bf16(x, g, tm):
    m, d = x.shape
    return pl.pallas_call(
        _rmsnorm_kernel,
        grid=(m // tm,),
        in_specs=[pl.BlockSpec((tm, d), lambda i: (i, 0)),
                  pl.BlockSpec((1, d), lambda i: (0, 0))],
        out_specs=pl.BlockSpec((tm, d), lambda i: (i, 0)),
        out_shape=jax.ShapeDtypeStruct((m, d), BF16),
        compiler_params=_params(("arbitrary",)),
        name="rmsnorm",
    )(x, g.reshape(1, d))


def _ffn_in_kernel(x_ref, wg_ref, wu_ref, wnext_ref, o_ref, wnext_bf16_ref):
    wg = wg_ref[...].astype(BF16)
    wu = wu_ref[...].astype(BF16)
    for c in range(x_ref.shape[0] // FFN_ROW_CHUNK):
        rows = slice(c * FFN_ROW_CHUNK, (c + 1) * FFN_ROW_CHUNK)
        x = x_ref[rows, :]
        gate = jnp.dot(x, wg, preferred_element_type=F32)
        up = jnp.dot(x, wu, preferred_element_type=F32)
        o_ref[rows, :] = (gate * jax.nn.sigmoid(gate) * up).astype(BF16)
    wnext_bf16_ref[...] = wnext_ref[...].astype(BF16)


def ffn_in(xn, w_all, w_next_all, layer, which, tm=2048, tn=512):
    m, d = xn.shape
    nj = D_FF // tn
    steps = (m // tm) * nj
    k_next, d_next = w_next_all.shape[-2:]
    slab = k_next // steps
    assert slab * steps == k_next and slab % 16 == 0
    return pl.pallas_call(
        _ffn_in_kernel,
        grid=(m // tm, nj),
        in_specs=[pl.BlockSpec((tm, d), lambda i, j: (i, 0)),
                  pl.BlockSpec((None, None, d, tn), lambda i, j: (layer, which, 0, j)),
                  pl.BlockSpec((None, None, d, tn), lambda i, j: (layer, which, 0, j + nj)),
                  pl.BlockSpec((None, None, slab, d_next), lambda i, j: (layer, which, i * nj + j, 0))],
        out_specs=[pl.BlockSpec((tm, tn), lambda i, j: (i, j)),
                   pl.BlockSpec((slab, d_next), lambda i, j: (i * nj + j, 0))],
        out_shape=[jax.ShapeDtypeStruct((m, D_FF), BF16),
                   jax.ShapeDtypeStruct((k_next, d_next), BF16)],
        compiler_params=_params(("arbitrary", "arbitrary")),
        name="ffn_in",
    )(xn, w_all, w_all, w_next_all)


OUT_ROW_CHUNK = 128


def _residual_epilogue(f, rows, x_ref, gpost_ref, gnext_ref, xo_ref, xn_ref, coef):
    xnew = x_ref[rows, :] + coef * _rms(f, gpost_ref[...])
    xo_ref[rows, :] = xnew
    xn_ref[rows, :] = _rms(xnew, gnext_ref[...]).astype(BF16)


def _ffn_out_kernel(*refs, coef, with_cast):
    a_ref, w_ref, x_ref, gpost_ref, gnext_ref = refs[:5]
    if with_cast:
        wnext_ref, xo_ref, xn_ref, wnext_bf16_ref = refs[5:]
        wnext_bf16_ref[...] = wnext_ref[...].astype(BF16)
    else:
        xo_ref, xn_ref = refs[5:]
    for c in range(a_ref.shape[0] // OUT_ROW_CHUNK):
        rows = slice(c * OUT_ROW_CHUNK, (c + 1) * OUT_ROW_CHUNK)
        f = jnp.dot(a_ref[rows, :], w_ref[...], preferred_element_type=F32)
        _residual_epilogue(f, rows, x_ref, gpost_ref, gnext_ref, xo_ref, xn_ref, coef)


def ffn_out(a, w, x, g_post, g_next, coef, w_next_all=None, layer=None, tm=256):
    m, k = a.shape
    d = w.shape[-1]
    n_tiles = m // tm
    row = lambda i: (i, 0)
    fixed = lambda i: (0, 0)
    in_specs = [pl.BlockSpec((tm, k), row),
                pl.BlockSpec((k, d), fixed, pipeline_mode=pl.Buffered(1)),
                pl.BlockSpec((tm, d), row),
                pl.BlockSpec((1, d), fixed),
                pl.BlockSpec((1, d), fixed)]
    out_specs = [pl.BlockSpec((tm, d), row), pl.BlockSpec((tm, d), row)]
    out_shape = [jax.ShapeDtypeStruct((m, d), F32), jax.ShapeDtypeStruct((m, d), BF16)]
    operands = [a, w, x, g_post.reshape(1, d), g_next.reshape(1, d)]
    with_cast = w_next_all is not None
    if with_cast:
        k_next, d_next = w_next_all.shape[-2:]
        slab = k_next // n_tiles
        assert slab * n_tiles == k_next and slab % 16 == 0
        in_specs.append(pl.BlockSpec((None, slab, d_next), lambda i: (layer, i, 0)))
        out_specs.append(pl.BlockSpec((slab, d_next), row))
        out_shape.append(jax.ShapeDtypeStruct((k_next, d_next), BF16))
        operands.append(w_next_all)
    return pl.pallas_call(
        functools.partial(_ffn_out_kernel, coef=coef, with_cast=with_cast),
        grid=(n_tiles,),
        in_specs=in_specs,
        out_specs=out_specs,
        out_shape=out_shape,
        compiler_params=_params(("arbitrary",)),
        name="ffn_out",
    )(*operands)


def _out_proj_kernel(s_ref, m_ref, w_ref, x_ref, gpost_ref, gnext_ref, xo_ref, xn_ref):
    chunk = OUT_ROW_CHUNK
    for c in range(m_ref.shape[0] // chunk):
        rows = slice(c * chunk, (c + 1) * chunk)
        cat = jnp.concatenate([s_ref[g, rows, :] for g in range(N_GROUPS)] + [m_ref[rows, :]], axis=-1)
        f = jnp.dot(cat, w_ref[...], preferred_element_type=F32)
        _residual_epilogue(f, rows, x_ref, gpost_ref, gnext_ref, xo_ref, xn_ref, 1.0)


def out_proj(self_out, mem_out, w, x, g_post, g_next, tm=512):
    m = x.shape[0]
    d = w.shape[-1]
    row = lambda i: (i, 0)
    fixed = lambda i: (0, 0)
    return pl.pallas_call(
        _out_proj_kernel,
        grid=(m // tm,),
        in_specs=[pl.BlockSpec((N_GROUPS, tm, GROUP_WIDTH), lambda i: (0, i, 0)),
                  pl.BlockSpec((tm, MEM_WIDTH), row),
                  pl.BlockSpec((SELF_WIDTH + MEM_WIDTH, d), fixed, pipeline_mode=pl.Buffered(1)),
                  pl.BlockSpec((tm, d), row),
                  pl.BlockSpec((1, d), fixed),
                  pl.BlockSpec((1, d), fixed)],
        out_specs=[pl.BlockSpec((tm, d), row), pl.BlockSpec((tm, d), row)],
        out_shape=[jax.ShapeDtypeStruct((m, d), F32), jax.ShapeDtypeStruct((m, d), BF16)],
        compiler_params=_params(("arbitrary",)),
        name="out_proj",
    )(self_out, mem_out, w, x, g_post.reshape(1, d), g_next.reshape(1, d))


Q_TILES = SELF_WIDTH // PROJ_TN
K_TILE0 = Q_TILES
V_TILE0 = 2 * Q_TILES
MQ_TILE = 3 * Q_TILES
N_PROJ_TILES = IN_WIDTH // PROJ_TN


def _qkv_proj_kernel(x_ref, w_ref, q_ref, k_ref, vt_ref):
    j = pl.program_id(1)
    tm = x_ref.shape[0]
    w = w_ref[...].astype(BF16)

    def row_chunks():
        for c in range(tm // FFN_ROW_CHUNK):
            rows = slice(c * FFN_ROW_CHUNK, (c + 1) * FFN_ROW_CHUNK)
            yield c, rows, jnp.dot(x_ref[rows, :], w, preferred_element_type=F32)

    def head_cols(r, h):
        return r[:, h * HEAD_DIM:(h + 1) * HEAD_DIM]

    @pl.when((j < K_TILE0) | (j == MQ_TILE))
    def _():
        for _, rows, r in row_chunks():
            for h in range(HEADS_PER_TILE):
                q_ref[h, rows, :] = head_cols(r, h).astype(BF16)

    @pl.when((j >= K_TILE0) & (j < V_TILE0))
    def _():
        for _, rows, r in row_chunks():
            for h in range(HEADS_PER_TILE):
                k_ref[h, rows, :] = head_cols(r, h).astype(BF16)

    @pl.when((j >= V_TILE0) & (j < MQ_TILE))
    def _():
        blocks_per_chunk = FFN_ROW_CHUNK // BLK
        for c, _, r in row_chunks():
            for h in range(HEADS_PER_TILE):
                for b in range(blocks_per_chunk):
                    blk = head_cols(r, h)[b * BLK:(b + 1) * BLK, :]
                    vt_ref[h, c * blocks_per_chunk + b, 0:HEAD_DIM, :] = blk.T.astype(BF16)
                    vt_ref[h, c * blocks_per_chunk + b, HEAD_DIM:VT_ROWS, :] = jnp.ones(
                        (VT_ROWS - HEAD_DIM, BLK), BF16)


def qkv_proj(xn, w_all, layer, tm=2048):
    m, d = xn.shape

    def q_map(i, j):
        return (jnp.where(j < K_TILE0, j, jnp.where(j < MQ_TILE, K_TILE0 - 1, K_TILE0)), i, 0)

    def k_map(i, j):
        return (jnp.clip(j - K_TILE0, 0, Q_TILES - 1), i, 0)

    def v_map(i, j):
        return (jnp.clip(j - V_TILE0, 0, Q_TILES - 1), i, 0, 0)

    hp = HEADS_PER_TILE
    return pl.pallas_call(
        _qkv_proj_kernel,
        grid=(m // tm, N_PROJ_TILES),
        in_specs=[pl.BlockSpec((tm, d), lambda i, j: (i, 0)),
                  pl.BlockSpec((None, d, PROJ_TN), lambda i, j: (layer, 0, j))],
        out_specs=[pl.BlockSpec((hp, tm, HEAD_DIM), q_map),
                   pl.BlockSpec((hp, tm, HEAD_DIM), k_map),
                   pl.BlockSpec((hp, tm // BLK, VT_ROWS, BLK), v_map)],
        out_shape=[jax.ShapeDtypeStruct((N_SELF_HEADS + N_MEM_HEADS, m, HEAD_DIM), BF16),
                   jax.ShapeDtypeStruct((N_SELF_HEADS, m, HEAD_DIM), BF16),
                   jax.ShapeDtypeStruct((N_SELF_HEADS, m // BLK, VT_ROWS, BLK), BF16)],
        compiler_params=_params(("arbitrary", "arbitrary")),
        name="qkv_proj",
    )(xn, w_all)


def _head_proj_kernel(x_ref, w_ref, o_ref):
    r = jnp.dot(x_ref[...], w_ref[...].astype(BF16), preferred_element_type=F32).astype(BF16)
    for c in range(o_ref.shape[0]):
        o_ref[c] = r[:, c * HEAD_DIM:(c + 1) * HEAD_DIM]


def mem_kv_proj(mem_n, w_all, layer):
    m, d = mem_n.shape
    n = w_all.shape[-1]
    return pl.pallas_call(
        _head_proj_kernel,
        grid=(n // PROJ_TN,),
        in_specs=[pl.BlockSpec((m, d), lambda j: (0, 0)),
                  pl.BlockSpec((None, d, PROJ_TN), lambda j: (layer, 0, j))],
        out_specs=pl.BlockSpec((HEADS_PER_TILE, m, HEAD_DIM), lambda j: (j, 0, 0)),
        out_shape=jax.ShapeDtypeStruct((n // HEAD_DIM, m, HEAD_DIM), BF16),
        compiler_params=_params(("arbitrary",)),
        name="mem_kv_proj",
    )(mem_n, w_all)


def _split3(x):
    hi = x.astype(BF16).astype(F32)
    mid = (x - hi).astype(BF16).astype(F32)
    lo = (x - hi - mid).astype(BF16).astype(F32)
    return hi, mid, lo


def _moba_augment(q_t, gate_t, slope, qb):
    blk_i = lax.broadcasted_iota(jnp.int32, (N_BLOCKS, BLK), 0)
    past = blk_i < qb
    gate_t = jnp.where(past, gate_t, NEG_INF)
    sel = jnp.zeros((N_BLOCKS, BLK), jnp.bool_)
    for _ in range(MOBA_TOP_K):
        best = jnp.max(gate_t, axis=0, keepdims=True)
        first = jnp.min(jnp.where(gate_t == best, blk_i, N_BLOCKS), axis=0, keepdims=True)
        chosen = blk_i == first
        sel = sel | chosen
        gate_t = jnp.where(chosen, -jnp.inf, gate_t)
    bias_t = jnp.where(sel & past, 0.0, NEG_INF / SCALE)
    hi, mid, lo = _split3(jnp.full((8, BLK), slope / SCALE, F32))
    sub = lax.broadcasted_iota(jnp.int32, (8, BLK), 0)
    parts = jnp.where(sub == 0, hi, jnp.where(sub == 1, mid, jnp.where(sub == 2, lo, 0.0)))
    pad = jnp.zeros((HEAD_DIM - N_BLOCKS - 8, BLK), F32)
    aug_t = jnp.concatenate([bias_t, parts, pad], axis=0)
    return jnp.concatenate([q_t, aug_t.astype(BF16)], axis=0)


def _moba_own_softmax(raw, slope):
    key_i = lax.broadcasted_iota(jnp.int32, (BLK, BLK), 0)
    qry_i = lax.broadcasted_iota(jnp.int32, (BLK, BLK), 1)
    s = SCALE * raw + slope * key_i.astype(F32)
    s = jnp.where(key_i <= qry_i, s, NEG_INF)
    m = jnp.max(s, axis=0, keepdims=True)
    return m, jnp.exp(s - m).astype(BF16)


def _moba_qk(q_aug_t, k_ref, kaug_ref, blk, raw_ref):
    rows = pl.ds(pl.multiple_of(blk * BLK, BLK), BLK)
    k_aug = jnp.concatenate([k_ref[rows, :], kaug_ref[rows, :]], axis=1)
    raw_ref[...] = jnp.dot(k_aug, q_aug_t, preferred_element_type=F32)


def _moba_softmax(raw_ref, p_ref, blk, slope, qb, m):
    shift = slope * ((blk - qb) * BLK).astype(F32)
    m_new = jnp.maximum(m, SCALE * jnp.max(raw_ref[...], axis=0, keepdims=True) + shift)
    alpha = jnp.exp(m - m_new)
    p = jnp.exp2((SCALE * LOG2E) * raw_ref[...] - LOG2E * (m_new - shift))
    p_ref[...] = p.astype(BF16)
    return m_new, alpha


def _moba_pv(p_ref, vt_ref, blk, alpha, acc_ref):
    acc_ref[...] = alpha * acc_ref[...] + jnp.dot(vt_ref[blk], p_ref[...], preferred_element_type=F32)


def _moba_kernel(slopes_ref, q_ref, k_ref, kaug_ref, vt_ref, o_ref, kmean_ref, raw_ref, p_ref, acc_ref):
    hp = pl.program_id(0)
    qb = pl.program_id(1)
    nh = MOBA_HEADS_PER_STEP
    last_blk = N_BLOCKS - 1

    @pl.when(qb == 0)
    def _():
        for hh in range(nh):
            def mean_block(j, carry):
                rows = pl.ds(pl.multiple_of(j * BLK, BLK), BLK)
                kmean_ref[hh, pl.ds(j, 1), :] = jnp.mean(
                    k_ref[hh, rows, :].astype(F32), axis=0, keepdims=True)
                return carry
            lax.fori_loop(0, N_BLOCKS, mean_block, 0)

    own_rows = pl.ds(pl.multiple_of(qb * BLK, BLK), BLK)
    slopes = [slopes_ref[hp * nh + hh] for hh in range(nh)]
    qs = [_transposed(q_ref[hh]) for hh in range(nh)]
    gates = [jnp.dot(kmean_ref[hh].astype(BF16), qs[hh], preferred_element_type=F32)
             for hh in range(nh)]
    own_raws = [jnp.dot(k_ref[hh, own_rows, :], qs[hh], preferred_element_type=F32)
                for hh in range(nh)]
    q_augs = [_moba_augment(qs[hh], gates[hh], slopes[hh], qb) for hh in range(nh)]
    owns = [_moba_own_softmax(own_raws[hh], slopes[hh]) for hh in range(nh)]
    for hh in range(nh):
        acc_ref[hh] = jnp.dot(vt_ref[hh, qb], owns[hh][1], preferred_element_type=F32)
        _moba_qk(q_augs[hh], k_ref.at[hh], kaug_ref, 0, raw_ref.at[0, hh])
        p_ref[1, hh] = jnp.zeros((BLK, BLK), BF16)
    states = [owns[hh][0] for hh in range(nh)] + [jnp.ones((1, BLK), F32) for _ in range(nh)]

    def body(c, carry):
        ms, alphas_odd = list(carry[:nh]), list(carry[nh:])
        even, odd = 2 * c, 2 * c + 1
        for hh in range(nh):
            k_h, vt_h, acc_h = k_ref.at[hh], vt_ref.at[hh], acc_ref.at[hh]
            _moba_pv(p_ref.at[1, hh], vt_h, jnp.maximum(even - 1, 0), alphas_odd[hh], acc_h)
            ms[hh], alpha_even = _moba_softmax(raw_ref.at[0, hh], p_ref.at[0, hh], even, slopes[hh], qb, ms[hh])
            _moba_qk(q_augs[hh], k_h, kaug_ref, odd, raw_ref.at[1, hh])
            _moba_pv(p_ref.at[0, hh], vt_h, even, alpha_even, acc_h)
            ms[hh], alphas_odd[hh] = _moba_softmax(raw_ref.at[1, hh], p_ref.at[1, hh], odd, slopes[hh], qb,
                                                   ms[hh])
            _moba_qk(q_augs[hh], k_h, kaug_ref, jnp.minimum(even + 2, last_blk), raw_ref.at[0, hh])
        return tuple(ms + alphas_odd)

    n_pairs = (qb + 1) // 2
    states = lax.fori_loop(0, n_pairs, body, tuple(states))
    for hh in range(nh):
        _moba_pv(p_ref.at[1, hh], vt_ref.at[hh], jnp.maximum(2 * n_pairs - 1, 0), states[nh + hh],
                 acc_ref.at[hh])
    for hh in range(nh):
        out_t = acc_ref[hh, 0:HEAD_DIM, :] / acc_ref[hh, HEAD_DIM:HEAD_DIM + 1, :]
        o_ref[:, hh * HEAD_DIM:(hh + 1) * HEAD_DIM] = out_t.T.astype(BF16)


def moba_key_augmentation():
    pos = lax.broadcasted_iota(jnp.int32, (SEQ, HEAD_DIM), 0)
    lane = lax.broadcasted_iota(jnp.int32, (SEQ, HEAD_DIM), 1)
    is_offset = (lane >= OFFSET_LANE) & (lane < OFFSET_LANE + 3)
    table = jnp.where(lane == pos // BLK, 1.0, jnp.where(is_offset, (pos % BLK).astype(F32), 0.0))
    return table.astype(BF16)


def moba_attention(q_heads, keys, v_t, slopes):
    nh = MOBA_HEADS_PER_STEP
    steps_per_group = HEADS_PER_GROUP // nh
    return pl.pallas_call(
        _moba_kernel,
        grid=(N_SELF_HEADS // nh, N_BLOCKS),
        in_specs=[pl.BlockSpec(memory_space=pltpu.SMEM),
                  pl.BlockSpec((nh, BLK, HEAD_DIM), lambda h, i: (h, i, 0)),
                  pl.BlockSpec((nh, SEQ, HEAD_DIM), lambda h, i: (h, 0, 0)),
                  pl.BlockSpec((SEQ, HEAD_DIM), lambda h, i: (0, 0), pipeline_mode=pl.Buffered(1)),
                  pl.BlockSpec((nh, N_BLOCKS, VT_ROWS, BLK), lambda h, i: (h, 0, 0, 0))],
        out_specs=pl.BlockSpec((None, BLK, nh * HEAD_DIM),
                               lambda h, i: (h // steps_per_group, i, h % steps_per_group)),
        out_shape=jax.ShapeDtypeStruct((N_GROUPS, SEQ, GROUP_WIDTH), BF16),
        scratch_shapes=[pltpu.VMEM((nh, N_BLOCKS, HEAD_DIM), F32),
                        pltpu.VMEM((2, nh, BLK, BLK), F32),
                        pltpu.VMEM((2, nh, BLK, BLK), BF16),
                        pltpu.VMEM((nh, VT_ROWS, BLK), F32)],
        compiler_params=_params(("arbitrary", "arbitrary")),
        name="moba_attn",
    )(slopes, q_heads, keys, moba_key_augmentation(), v_t)


def _blocks_back(window):
    return max(window // BLK, 1)


DILATED_HEADS_PER_STEP = 2


def _dilated_tiles():
    return [(g, hh, t) for g, (window, _) in enumerate(DILATED_CONFIGS)
            for t in range(_blocks_back(window) + 1) for hh in range(DILATED_HEADS_PER_STEP)]


def _dilated_kernel(slopes_ref, *refs):
    q_refs, k_refs, vt_refs, o_ref = refs[0:3], refs[3:6], refs[6:9], refs[9]
    bias_refs = refs[10:13]
    hp = pl.program_id(0)
    qb = pl.program_id(1)
    nh = DILATED_HEADS_PER_STEP

    @pl.when(qb == 0)
    def _():
        key_i = lax.broadcasted_iota(jnp.int32, (BLK, BLK), 0)
        qry_i = lax.broadcasted_iota(jnp.int32, (BLK, BLK), 1)
        for g, (window, dilation) in enumerate(DILATED_CONFIGS):
            n_back = _blocks_back(window)
            for delta in range(n_back + 1):
                dist = delta * BLK + qry_i - key_i
                ok = (dist >= 0) & (dist <= window) & ((dist & (dilation - 1)) == 0)
                for hh in range(nh):
                    slope = slopes_ref[g * HEADS_PER_GROUP + hp * nh + hh]
                    bias_refs[g][hh, delta] = jnp.where(ok, (-LOG2E * slope) * dist.astype(F32), NEG_INF)
            for hh in range(nh):
                bias_refs[g][hh, n_back + 1] = jnp.full((BLK, BLK), NEG_INF, F32)

    qs = [[q_refs[g][hh] for hh in range(nh)] for g in range(N_GROUPS)]
    tiles = _dilated_tiles()

    def block_of(g, t):
        n_back = _blocks_back(DILATED_CONFIGS[g][0])
        j = qb - n_back + t
        table = jnp.where(j >= 0, n_back - t, n_back + 1)
        return jnp.maximum(j, 0), table

    def score(g, hh, t):
        jc, table = block_of(g, t)
        rows = pl.ds(pl.multiple_of(jc * BLK, BLK), BLK)
        return (SCALE * LOG2E) * _nt(k_refs[g][hh, rows, :], qs[g][hh]) + bias_refs[g][hh, table]

    def softmax(s2):
        bmax = jnp.max(s2, axis=0, keepdims=True)
        return bmax, jnp.exp2(s2 - bmax).astype(BF16)

    def pv(g, hh, t, p):
        jc, _ = block_of(g, t)
        return jnp.dot(vt_refs[g][hh, jc], p, preferred_element_type=F32)

    maxes = {(g, hh): [] for g in range(N_GROUPS) for hh in range(nh)}
    parts = {(g, hh): [] for g in range(N_GROUPS) for hh in range(nh)}
    s_next = score(*tiles[0])
    p_prev = None
    for i, (g, hh, t) in enumerate(tiles):
        s_cur = s_next
        if i + 1 < len(tiles):
            s_next = score(*tiles[i + 1])
        bmax, p_cur = softmax(s_cur)
        maxes[g, hh].append(bmax)
        if p_prev is not None:
            pg, ph, pt = tiles[i - 1]
            parts[pg, ph].append(pv(pg, ph, pt, p_prev))
        p_prev = p_cur
    pg, ph, pt = tiles[-1]
    parts[pg, ph].append(pv(pg, ph, pt, p_prev))

    for hh in range(nh):
        outs, lses = [], []
        for g in range(N_GROUPS):
            m2 = functools.reduce(jnp.maximum, maxes[g, hh])
            acc = None
            for bmax, part in zip(maxes[g, hh], parts[g, hh]):
                d = jnp.exp2(bmax - m2) * part
                acc = d if acc is None else acc + d
            l = acc[HEAD_DIM:HEAD_DIM + 1, :]
            outs.append(acc[0:HEAD_DIM, :] / l)
            lses.append(m2 * LN2 + jnp.log(l))
        mx = functools.reduce(jnp.maximum, lses)
        es = [jnp.exp(lse - mx) for lse in lses]
        tot = functools.reduce(lambda a, b: a + b, es)
        for g in range(N_GROUPS):
            o_ref[g, :, hh * HEAD_DIM:(hh + 1) * HEAD_DIM] = ((es[g] / tot) * outs[g]).T.astype(BF16)


def dilated_attention(q_heads, keys, v_t, slopes):
    nh = DILATED_HEADS_PER_STEP
    steps = HEADS_PER_GROUP // nh

    def q_spec(g):
        return pl.BlockSpec((nh, BLK, HEAD_DIM), lambda hp, i: (g * steps + hp, i, 0))

    def k_spec(g):
        return pl.BlockSpec((nh, SEQ, HEAD_DIM), lambda hp, i: (g * steps + hp, 0, 0),
                            pipeline_mode=pl.Buffered(1))

    def v_spec(g):
        return pl.BlockSpec((nh, N_BLOCKS, VT_ROWS, BLK), lambda hp, i: (g * steps + hp, 0, 0, 0),
                            pipeline_mode=pl.Buffered(1))

    in_specs = ([pl.BlockSpec(memory_space=pltpu.SMEM)]
                + [q_spec(g) for g in range(N_GROUPS)]
                + [k_spec(g) for g in range(N_GROUPS)]
                + [v_spec(g) for g in range(N_GROUPS)])
    scratch = [pltpu.VMEM((nh, _blocks_back(w) + 2, BLK, BLK), F32) for w, _ in DILATED_CONFIGS]
    return pl.pallas_call(
        _dilated_kernel,
        grid=(steps, N_BLOCKS),
        in_specs=in_specs,
        out_specs=pl.BlockSpec((N_GROUPS, BLK, nh * HEAD_DIM), lambda hp, i: (0, i, hp)),
        out_shape=jax.ShapeDtypeStruct((N_GROUPS, SEQ, GROUP_WIDTH), BF16),
        scratch_shapes=scratch,
        compiler_params=_params(("arbitrary", "arbitrary")),
        name="dilated_attn",
    )(slopes, *([q_heads] * 3), *([keys] * 3), *([v_t] * 3))


def _mem_attn_kernel(q_ref, k_ref, v_ref, o_ref):
    s = _nt(q_ref[...], k_ref[...]) * SCALE
    m = jnp.max(s, axis=-1, keepdims=True)
    p = jnp.exp(s - m)
    l = jnp.sum(p, axis=-1, keepdims=True)
    acc = jnp.dot(p.astype(BF16), v_ref[...], preferred_element_type=F32)
    o_ref[...] = (acc / l).astype(BF16)


def mem_attention(q_heads, mem_kv, tq=1024):
    return pl.pallas_call(
        _mem_attn_kernel,
        grid=(N_MEM_HEADS, SEQ // tq),
        in_specs=[pl.BlockSpec((None, tq, HEAD_DIM), lambda h, i: (N_SELF_HEADS + h, i, 0)),
                  pl.BlockSpec((None, N_MEM, HEAD_DIM), lambda h, i: (h, 0, 0)),
                  pl.BlockSpec((None, N_MEM, HEAD_DIM), lambda h, i: (N_MEM_HEADS + h, 0, 0))],
        out_specs=pl.BlockSpec((tq, HEAD_DIM), lambda h, i: (i, h)),
        out_shape=jax.ShapeDtypeStruct((SEQ, MEM_WIDTH), BF16),
        compiler_params=_params(("arbitrary", "arbitrary")),
        name="mem_attn",
    )(q_heads, mem_kv, mem_kv)


def kernel(x, mem, g_pre, g_post, g_mem, w_ffn_in, w_ffn_out, w_in, w_mem_kv, w_out):
    slopes = jnp.exp2(-8.0 * jnp.arange(1, N_SELF_HEADS + 1, dtype=F32) / N_SELF_HEADS)
    xs = x.reshape(SEQ, D_MODEL)
    mem2 = mem.reshape(N_MEM, D_MODEL)

    xn = rmsnorm_bf16(xs, g_pre[0, 0], tm=512)
    for layer in range(DEPTH):
        h, w_down = ffn_in(xn, w_ffn_in, w_ffn_out, layer, 0)
        xs, xn, w_o = ffn_out(h, w_down, xs, g_post[layer, 0], g_pre[layer, 1], 0.5, w_out, layer)

        q_heads, keys, v_t = qkv_proj(xn, w_in, layer)
        mem_n = rmsnorm_bf16(mem2, g_mem[layer], tm=N_MEM)
        mem_kv = mem_kv_proj(mem_n, w_mem_kv, layer)
        if layer % 2 == 0:
            self_out = moba_attention(q_heads, keys, v_t, slopes)
        else:
            self_out = dilated_attention(q_heads, keys, v_t, slopes)
        mem_out = mem_attention(q_heads, mem_kv)
        xs, xn = out_proj(self_out, mem_out, w_o, xs, g_post[layer, 1], g_pre[layer, 2])

        h, w_down = ffn_in(xn, w_ffn_in, w_ffn_out, layer, 1)
        g_next = g_pre[layer + 1, 0] if layer + 1 < DEPTH else g_pre[layer, 0]
        xs, xn = ffn_out(h, w_down, xs, g_post[layer, 2], g_next, 0.5)
    return xs.reshape(x.shape)
```
